```python
import jax, jax.numpy as jnp
from jax import lax
import numpy as np

D_MODEL = 4096
BATCH = 1
SEQ = 8192
DEPTH = 1

D_MIX = D_MODEL
D_GLA = D_MIX // 2
GLA_HEADS = 8
GLA_DV = D_GLA // GLA_HEADS
GLA_DK = GLA_DV // 2
GLA_KEY = GLA_HEADS * GLA_DK
GLA_GATE_RANK = 16
GLA_GATE_NORMALIZER = 16.0
D_HGRN = D_MIX - D_GLA
HGRN_EXPAND = 128
HGRN_HEADS = D_HGRN // HGRN_EXPAND
CHUNK = 64
N_EXPERTS = 16
D_EXPERT = 2048
CAPACITY_FACTOR = 2
EPS = 1e-6

SPLIT_SIZES = (
    GLA_KEY,
    GLA_KEY,
    D_GLA,
    GLA_GATE_RANK,
    GLA_GATE_RANK,
    D_GLA,
    D_HGRN,
    D_HGRN,
    D_HGRN,
    D_HGRN,
    D_HGRN,
)
D_IN_PROJ = sum(SPLIT_SIZES)

kernel_name = "hybrid_gla_hgrn2_ecmoe_encoder"


def rms_norm(x, w):
    xf = x.astype(jnp.float32)
    y = xf * lax.rsqrt(jnp.mean(xf * xf, axis=-1, keepdims=True) + EPS)
    return (y * w.astype(jnp.float32)).astype(x.dtype)


def to_heads(t, n_heads):
    b, l, hd = t.shape
    return t.reshape(b, l, n_heads, hd // n_heads).transpose(0, 2, 1, 3)


def from_heads(t):
    b, h, l, d = t.shape
    return t.transpose(0, 2, 1, 3).reshape(b, l, h * d)


def chunk_gated_linear_scan(q, k, v, log_a):
    b, h, l, dk = q.shape
    dv = v.shape[-1]
    n_chunks = l // CHUNK

    def chunks(t):
        return t.reshape(b, h, n_chunks, CHUNK, t.shape[-1]).transpose(2, 0, 1, 3, 4)

    lower = jnp.tril(jnp.ones((CHUNK, CHUNK), dtype=bool))

    def step(state, inp):
        qc, kc, vc, gc = (t.astype(jnp.float32) for t in inp)
        cum = jnp.cumsum(gc, axis=2)
        o_inter = jnp.einsum('bhtd,bhde->bhte', qc * jnp.exp(cum), state)
        rel = cum[:, :, :, None, :] - cum[:, :, None, :, :]
        decay = jnp.exp(jnp.where(lower[None, None, :, :, None], rel, -jnp.inf))
        scores = jnp.einsum('bhtd,bhsd,bhtsd->bhts', qc, kc, decay)
        o_intra = jnp.einsum('bhts,bhse->bhte', scores, vc)
        last = cum[:, :, -1:, :]
        k_dec = kc * jnp.exp(last - cum)
        new_state = (jnp.exp(last[:, :, 0, :])[..., None] * state
                     + jnp.einsum('bhsd,bhse->bhde', k_dec, vc))
        return new_state, o_inter + o_intra

    state0 = jnp.zeros((b, h, dk, dv), jnp.float32)
    _, o = lax.scan(step, state0, (chunks(q), chunks(k), chunks(v), chunks(log_a)))
    return o.transpose(1, 2, 0, 3, 4).reshape(b, h, l, dv).astype(v.dtype)


def bidirectional_scan(q, k_fwd, k_bwd, v, log_a_fwd, log_a_bwd):
    flip = lambda t: jnp.flip(t, axis=2)
    fwd = chunk_gated_linear_scan(q, k_fwd, v, log_a_fwd)
    bwd = flip(chunk_gated_linear_scan(flip(q), flip(k_bwd), flip(v), flip(log_a_bwd)))
    return fwd + bwd


def gla_group(q, k, v, gate_low_f, gate_low_b, og,
              up_f, bias_f, up_b, bias_b, norm_w):
    log_a_f = jax.nn.log_sigmoid(gate_low_f @ up_f + bias_f) / GLA_GATE_NORMALIZER
    log_a_b = jax.nn.log_sigmoid(gate_low_b @ up_b + bias_b) / GLA_GATE_NORMALIZER
    qh = to_heads(q, GLA_HEADS) * (GLA_DK ** -0.5)
    kh = to_heads(k, GLA_HEADS)
    vh = to_heads(v, GLA_HEADS)
    o = bidirectional_scan(qh, kh, kh, vh,
                           to_heads(log_a_f, GLA_HEADS), to_heads(log_a_b, GLA_HEADS))
    o = rms_norm(o, norm_w) * jax.nn.silu(to_heads(og, GLA_HEADS))
    return from_heads(o)


def hgrn2_group(q, f_raw_f, f_raw_b, i, og, lb_f, lb_b, norm_w):
    def gates(f_raw, lb):
        k = (1.0 - lb) * jax.nn.sigmoid(-f_raw)
        log_f = jnp.log(lb + (1.0 - lb) * jax.nn.sigmoid(f_raw))
        return to_heads(k, HGRN_HEADS), to_heads(log_f, HGRN_HEADS)
    k_f, log_f_f = gates(f_raw_f, lb_f)
    k_b, log_f_b = gates(f_raw_b, lb_b)
    qh = to_heads(jax.nn.silu(q), HGRN_HEADS) * (HGRN_EXPAND ** -0.5)
    ih = to_heads(i, HGRN_HEADS)
    o = bidirectional_scan(qh, k_f, k_b, ih, log_f_f, log_f_b)
    o = rms_norm(o, norm_w) * jax.nn.silu(to_heads(og, HGRN_HEADS))
    return from_heads(o)


def expert_choice_ffn(x, w_router, w1, w3, w2):
    b, n, d = x.shape
    cap = CAPACITY_FACTOR * n // N_EXPERTS
    affinity = jax.nn.softmax((x @ w_router).astype(jnp.float32), axis=-1)
    gate, idx = lax.top_k(affinity.transpose(0, 2, 1), cap)
    xe = jax.vmap(lambda xb, ib: xb[ib])(x, idx)
    hidden = (jax.nn.silu(jnp.einsum('becd,edf->becf', xe, w1))
              * jnp.einsum('becd,edf->becf', xe, w3))
    ye = jnp.einsum('becf,efd->becd', hidden, w2) * gate[..., None].astype(x.dtype)
    return jax.vmap(
        lambda yb, ib: jnp.zeros((n, d), yb.dtype).at[ib.reshape(-1)].add(yb.reshape(-1, d))
    )(ye, idx)


def setup_inputs(seed: int = 0) -> dict:
    key = jax.random.key(seed)
    ks = jax.random.split(key, 24)

    def normal(k, shape, fan_in):
        return jax.random.normal(k, shape, jnp.float32) * (fan_in ** -0.5)

    def gain(k, shape):
        return 1.0 + 0.02 * jax.random.normal(k, shape, jnp.float32)

    return {
        "x": jax.random.normal(ks[0], (BATCH, SEQ, D_MODEL), jnp.float32),
        "norm_mix_w": gain(ks[1], (DEPTH, D_MODEL)),
        "w_in": normal(ks[2], (DEPTH, D_MODEL, D_IN_PROJ), D_MODEL),
        "gla_gate_up_f": normal(ks[3], (DEPTH, GLA_GATE_RANK, GLA_KEY), GLA_GATE_RANK),
        "gla_gate_bias_f": 0.1 * jax.random.normal(ks[4], (DEPTH, GLA_KEY), jnp.float32),
        "gla_gate_up_b": normal(ks[5], (DEPTH, GLA_GATE_RANK, GLA_KEY), GLA_GATE_RANK),
        "gla_gate_bias_b": 0.1 * jax.random.normal(ks[6], (DEPTH, GLA_KEY), jnp.float32),
        "gla_norm_w": gain(ks[7], (DEPTH, GLA_DV)),
        "hgrn_lb_logits_f": 0.5 * jax.random.normal(ks[8], (DEPTH + 1, D_HGRN), jnp.float32),
        "hgrn_lb_logits_b": 0.5 * jax.random.normal(ks[9], (DEPTH + 1, D_HGRN), jnp.float32),
        "hgrn_norm_w": gain(ks[10], (DEPTH, HGRN_EXPAND)),
        "w_out": normal(ks[11], (DEPTH, D_MIX, D_MODEL), D_MIX),
        "norm_ffn_w": gain(ks[12], (DEPTH, D_MODEL)),
        "w_router": normal(ks[13], (DEPTH, D_MODEL, N_EXPERTS), D_MODEL),
        "expert_w1": normal(ks[14], (DEPTH, N_EXPERTS, D_MODEL, D_EXPERT), D_MODEL),
        "expert_w3": normal(ks[15], (DEPTH, N_EXPERTS, D_MODEL, D_EXPERT), D_MODEL),
        "expert_w2": normal(ks[16], (DEPTH, N_EXPERTS, D_EXPERT, D_MODEL), D_EXPERT),
        "norm_final_w": gain(ks[17], (D_MODEL,)),
    }


def reference(x, norm_mix_w, w_in, gla_gate_up_f, gla_gate_bias_f, gla_gate_up_b,
              gla_gate_bias_b, gla_norm_w, hgrn_lb_logits_f, hgrn_lb_logits_b,
              hgrn_norm_w, w_out, norm_ffn_w, w_router, expert_w1, expert_w3,
              expert_w2, norm_final_w):
    split_points = [int(p) for p in np.cumsum(SPLIT_SIZES)[:-1]]
    lb_table_f = jnp.cumsum(jax.nn.softmax(hgrn_lb_logits_f.astype(jnp.float32), axis=0), axis=0)
    lb_table_b = jnp.cumsum(jax.nn.softmax(hgrn_lb_logits_b.astype(jnp.float32), axis=0), axis=0)
    for layer in range(DEPTH):
        xn = rms_norm(x, norm_mix_w[layer])
        proj = xn @ w_in[layer]
        (g_q, g_k, g_v, g_low_f, g_low_b, g_og,
         h_q, h_f_f, h_f_b, h_i, h_og) = jnp.split(proj, split_points, axis=-1)
        gla_out = gla_group(g_q, g_k, g_v, g_low_f, g_low_b, g_og,
                            gla_gate_up_f[layer], gla_gate_bias_f[layer],
                            gla_gate_up_b[layer], gla_gate_bias_b[layer],
                            gla_norm_w[layer])
        hgrn_out = hgrn2_group(h_q, h_f_f, h_f_b, h_i, h_og,
                               lb_table_f[layer], lb_table_b[layer], hgrn_norm_w[layer])
        mixed = jnp.concatenate([gla_out, hgrn_out], axis=-1)
        x = x + mixed @ w_out[layer]
        xn = rms_norm(x, norm_ffn_w[layer])
        x = x + expert_choice_ffn(xn, w_router[layer], expert_w1[layer],
                                  expert_w3[layer], expert_w2[layer])
    return rms_norm(x, norm_final_w)
```

```python
import functools

import jax
import jax.numpy as jnp
from jax import lax
from jax.experimental import pallas as pl
from jax.experimental.pallas import tpu as pltpu

F32 = jnp.float32
BF16 = jnp.bfloat16
I32 = jnp.int32

EPS = 1e-6
LANES = 128
BF16_SUBLANES = 16
VMEM_LIMIT = 56 * 1024 * 1024

GLA_HEADS = 8
GLA_DK = 128
GLA_DV = 256
GLA_GATE_RANK = 16
GLA_GATE_NORMALIZER = 16.0
HGRN_HEADS = 16
HGRN_D = 128
CHUNK = 64
SUB = 16
N_EXPERTS = 16
CAPACITY_FACTOR = 2

NT_DIMS = (((1,), (1,)), ((), ()))
TN_DIMS = (((0,), (0,)), ((), ()))


def _params(*sem):
    return pltpu.CompilerParams(dimension_semantics=sem, vmem_limit_bytes=VMEM_LIMIT)


def _rmsnorm_kernel(x_ref, w_ref, o_ref):
    x = x_ref[...]
    ms = jnp.mean(x * x, axis=-1, keepdims=True)
    o_ref[...] = (x * lax.rsqrt(ms + EPS) * w_ref[...]).astype(o_ref.dtype)


def rmsnorm(x, w, out_dtype, tm=256):
    n, d = x.shape
    return pl.pallas_call(
        _rmsnorm_kernel,
        grid=(n // tm,),
        in_specs=[pl.BlockSpec((tm, d), lambda i: (i, 0)),
                  pl.BlockSpec((1, d), lambda i: (0, 0))],
        out_specs=pl.BlockSpec((tm, d), lambda i: (i, 0)),
        out_shape=jax.ShapeDtypeStruct((n, d), out_dtype),
        compiler_params=_params("parallel"),
        name="rmsnorm",
    )(x, w.reshape(1, d))


def _mm_kernel(a_ref, b_ref, o_ref):
    o_ref[...] = jnp.dot(a_ref[...], b_ref[...], preferred_element_type=F32)


def matmul(a, b, bm, bn, name):
    m, k = a.shape
    _, n = b.shape
    bm, bn = min(bm, m), min(bn, n)
    return pl.pallas_call(
        _mm_kernel,
        grid=(m // bm, n // bn),
        in_specs=[pl.BlockSpec((bm, k), lambda i, j: (i, 0)),
                  pl.BlockSpec((k, bn), lambda i, j: (0, j))],
        out_specs=pl.BlockSpec((bm, bn), lambda i, j: (i, j)),
        out_shape=jax.ShapeDtypeStruct((m, n), F32),
        compiler_params=_params("parallel", "parallel"),
        name=name,
    )(a, b)


def _scan_kernel(*refs, kind, rev, tb, dk, dv):
    if kind == "gla":
        q_ref, k_ref, v_ref, low_ref, u_ref, b_ref = refs[:6]
        rest = refs[6:]
    else:
        q_ref, f_ref, v_ref, lbl_ref = refs[:4]
        rest = refs[4:]
    if rev:
        prev_ref, og_ref, nw_ref, o_ref = rest[:4]
        scr = rest[4:]
    else:
        o_ref = rest[0]
        scr = rest[1:]
    st_ref, q_s, k_s, cum_s, a_s, o_s = scr

    @pl.when(pl.program_id(1) == 0)
    def _():
        st_ref[...] = jnp.zeros_like(st_ref)

    scale = dk ** -0.5
    if kind == "gla":
        z = jnp.dot(low_ref[...].astype(BF16), u_ref[...],
                    preferred_element_type=F32) + b_ref[...]
        log_sig = jnp.minimum(z, 0.0) - jnp.log1p(jnp.exp(-jnp.abs(z)))
        g = log_sig * (1.0 / GLA_GATE_NORMALIZER)
        q_s[...] = q_ref[...] * scale
        k_s[...] = k_ref[...]
    else:
        logits = lbl_ref[...]
        e = jnp.exp(logits - jnp.max(logits, axis=0, keepdims=True))
        lb = e[0:1] / jnp.sum(e, axis=0, keepdims=True)
        f = f_ref[...]
        k_s[...] = (1.0 - lb) * jax.nn.sigmoid(-f)
        g = jnp.log(lb + (1.0 - lb) * jax.nn.sigmoid(f))
        qq = q_ref[...]
        q_s[...] = qq * jax.nn.sigmoid(qq) * scale

    c = CHUNK
    n_sub = c // SUB
    row_cc = lax.broadcasted_iota(I32, (c, c), 0)
    col_cc = lax.broadcasted_iota(I32, (c, c), 1)
    tri = (col_cc >= row_cc) if rev else (col_cc <= row_cc)
    tri_b = jnp.where(tri, 1.0, 0.0).astype(BF16)
    row_ck = lax.broadcasted_iota(I32, (c, dk), 0)
    lane_sc = lax.broadcasted_iota(I32, (SUB, c), 1)
    t_loc = lax.broadcasted_iota(I32, (SUB, 1), 0)

    n_chunk = tb // c
    order = range(n_chunk - 1, -1, -1) if rev else range(n_chunk)
    for j in order:
        r0 = j * c
        gc = g[r0:r0 + c]
        g_hi = gc.astype(BF16)
        res = gc - g_hi.astype(F32)
        g_mid = res.astype(BF16)
        g_lo = (res - g_mid.astype(F32)).astype(BF16)
        cum = (jnp.dot(tri_b, g_hi, preferred_element_type=F32)
               + jnp.dot(tri_b, g_mid, preferred_element_type=F32)
               + jnp.dot(tri_b, g_lo, preferred_element_type=F32))
        cum_s[...] = cum
        qc = q_s[r0:r0 + c]
        kc = k_s[r0:r0 + c]
        vb = v_ref[r0:r0 + c].astype(BF16)
        tot = cum[0:1] if rev else cum[c - 1:c]

        st = st_ref[...]
        qh = (qc * jnp.exp(cum)).astype(BF16)
        o_inter = lax.dot_general(qh, st.astype(BF16), NT_DIMS,
                                  preferred_element_type=F32)
        kdec = (kc * jnp.exp(tot - cum)).astype(BF16)
        st_ref[...] = jnp.exp(tot) * st + lax.dot_general(
            vb, kdec, TN_DIMS, preferred_element_type=F32)

        blocks = []
        for j1 in range(n_sub):
            lo, hi = j1 * SUB, (j1 + 1) * SUB
            if (not rev and j1 == 0) or (rev and j1 == n_sub - 1):
                blocks.append(jnp.zeros((SUB, c), F32))
                continue
            if rev:
                ref_row = cum[hi:hi + 1]
                valid = row_ck >= hi
            else:
                ref_row = cum[lo - 1:lo]
                valid = row_ck < lo
            qt = (qc[lo:hi] * jnp.exp(cum[lo:hi] - ref_row)).astype(BF16)
            kt = (kc * jnp.exp(jnp.where(valid, ref_row - cum, -jnp.inf))).astype(BF16)
            blocks.append(lax.dot_general(qt, kt, NT_DIMS, preferred_element_type=F32))
        a_s[...] = jnp.concatenate(blocks, axis=0)

        def diag_body(s, carry, r0=r0):
            base = pl.multiple_of((s // SUB) * SUB, SUB)
            cs = cum_s[pl.ds(s, 1), :]
            ks = k_s[pl.ds(r0 + s, 1), :]
            qb = q_s[pl.ds(r0 + base, SUB), :]
            cb = cum_s[pl.ds(base, SUB), :]
            t_abs = t_loc + base
            valid_t = (t_abs <= s) if rev else (t_abs >= s)
            p = qb * ks * jnp.exp(jnp.where(valid_t, cb - cs, -jnp.inf))
            col = jnp.sum(p, axis=1, keepdims=True)
            blk = a_s[pl.ds(base, SUB), :]
            a_s[pl.ds(base, SUB), :] = jnp.where(lane_sc == s, col, blk)
            return carry

        lax.fori_loop(0, c, diag_body, 0)
        o_intra = jnp.dot(a_s[...].astype(BF16), vb, preferred_element_type=F32)
        o_s[r0:r0 + c, :] = o_inter + o_intra

    o = o_s[...]
    if rev:
        t = prev_ref[...] + o
        ms = jnp.mean(t * t, axis=-1, keepdims=True)
        y = t * lax.rsqrt(ms + EPS) * nw_ref[...]
        og = og_ref[...]
        o_ref[...] = (y * (og * jax.nn.sigmoid(og))).astype(o_ref.dtype)
    else:
        o_ref[...] = o


def scan(kind, rev, proj, extras, heads, dk, dv, cols, tb=256):
    n = proj.shape[0]
    nb = n // tb

    def blk(c):
        return (nb - 1 - c) if rev else c

    def col_spec(width, off):
        base = off // width
        return pl.BlockSpec((tb, width), lambda h, c: (blk(c), base + h))

    if kind == "gla":
        low, u, bias = extras["low"], extras["u"], extras["bias"]
        operands = [proj, proj, proj, low, u, bias]
        in_specs = [col_spec(dk, cols["q"]), col_spec(dk, cols["k"]), col_spec(dv, cols["v"]),
                    pl.BlockSpec((tb, LANES), lambda h, c: (blk(c), 0)),
                    pl.BlockSpec((LANES, dk), lambda h, c: (0, h)),
                    pl.BlockSpec((1, dk), lambda h, c: (0, h))]
    else:
        lbl = extras["lb_logits"]
        operands = [proj, proj, proj, lbl]
        in_specs = [col_spec(dk, cols["q"]), col_spec(dk, cols["f"]), col_spec(dv, cols["v"]),
                    pl.BlockSpec((lbl.shape[0], dk), lambda h, c: (0, h))]
    if rev:
        operands += [extras["prev"], proj, extras["norm_w"]]
        in_specs += [pl.BlockSpec((tb, dv), lambda h, c: (blk(c), h)),
                     col_spec(dv, cols["og"]),
                     pl.BlockSpec((1, dv), lambda h, c: (0, 0))]
    out_dtype = BF16 if rev else F32
    kern = functools.partial(_scan_kernel, kind=kind, rev=rev, tb=tb, dk=dk, dv=dv)
    return pl.pallas_call(
        kern,
        grid=(heads, nb),
        in_specs=in_specs,
        out_specs=pl.BlockSpec((tb, dv), lambda h, c: (blk(c), h)),
        out_shape=jax.ShapeDtypeStruct((n, heads * dv), out_dtype),
        scratch_shapes=[pltpu.VMEM((dv, dk), F32),
                        pltpu.VMEM((tb, dk), F32),
                        pltpu.VMEM((tb, dk), F32),
                        pltpu.VMEM((CHUNK, dk), F32),
                        pltpu.VMEM((CHUNK, CHUNK), F32),
                        pltpu.VMEM((tb, dv), F32)],
        compiler_params=_params("parallel", "arbitrary"),
        name=f"scan_{kind}_{'bwd' if rev else 'fwd'}",
    )(*operands)


def _outproj_kernel(a_ref, b_ref, wa_ref, wb_ref, x_ref, o_ref):
    acc = jnp.dot(a_ref[...], wa_ref[...], preferred_element_type=F32)
    acc += jnp.dot(b_ref[...], wb_ref[...], preferred_element_type=F32)
    o_ref[...] = x_ref[...] + acc


def outproj(a, b, w, x, bm=512, bn=1024):
    m, ka = a.shape
    kb = b.shape[1]
    n = w.shape[1]
    assert ka == kb
    bm, bn = min(bm, m), min(bn, n)
    return pl.pallas_call(
        _outproj_kernel,
        grid=(n // bn, m // bm),
        in_specs=[pl.BlockSpec((bm, ka), lambda j, i: (i, 0)),
                  pl.BlockSpec((bm, kb), lambda j, i: (i, 0)),
                  pl.BlockSpec((ka, bn), lambda j, i: (0, j)),
                  pl.BlockSpec((kb, bn), lambda j, i: (1, j)),
                  pl.BlockSpec((bm, bn), lambda j, i: (i, j))],
        out_specs=pl.BlockSpec((bm, bn), lambda j, i: (i, j)),
        out_shape=jax.ShapeDtypeStruct((m, n), F32),
        compiler_params=_params("parallel", "parallel"),
        name="outproj",
    )(a, b, w, w, x)


def _norm_router_kernel(h_ref, w_ref, wr_ref, hn_ref, aff_ref):
    x = h_ref[...]
    ms = jnp.mean(x * x, axis=-1, keepdims=True)
    hn = x * lax.rsqrt(ms + EPS) * w_ref[...]
    hn_ref[...] = hn
    logits = lax.dot_general(wr_ref[...], hn, NT_DIMS, preferred_element_type=F32,
                             precision=lax.Precision.HIGHEST)
    m = jnp.max(logits, axis=0, keepdims=True)
    p = jnp.exp(logits - m)
    aff_ref[...] = p / jnp.sum(p, axis=0, keepdims=True)


def norm_router(h, w, w_router_t, tm=256):
    n, d = h.shape
    e = w_router_t.shape[0]
    return pl.pallas_call(
        _norm_router_kernel,
        grid=(n // tm,),
        in_specs=[pl.BlockSpec((tm, d), lambda i: (i, 0)),
                  pl.BlockSpec((1, d), lambda i: (0, 0)),
                  pl.BlockSpec((e, d), lambda i: (0, 0))],
        out_specs=[pl.BlockSpec((tm, d), lambda i: (i, 0)),
                   pl.BlockSpec((e, tm), lambda i: (0, i))],
        out_shape=[jax.ShapeDtypeStruct((n, d), F32),
                   jax.ShapeDtypeStruct((e, n), F32)],
        compiler_params=_params("parallel"),
        name="norm_router",
    )(h, w.reshape(1, d), w_router_t)


def _prefix_excl(m_f, out_ref, tri_b):
    e, n = m_f.shape
    off = jnp.zeros((e, 1), F32)
    for b in range(n // LANES):
        mb = m_f[:, b * LANES:(b + 1) * LANES]
        inc = jnp.dot(mb.astype(BF16), tri_b, preferred_element_type=F32)
        out_ref[:, b * LANES:(b + 1) * LANES] = inc - mb + off
        off = off + inc[:, LANES - 1:LANES]


def _select_kernel(aff_ref, selrank_ref, cnt_ref, eqr_s, *, cap):
    a = aff_ref[...]
    key = pltpu.bitcast(a, I32)
    e, n = a.shape
    thr = jnp.zeros((e, 1), I32)
    for b in range(30, -1, -1):
        cand = thr | (1 << b)
        cnt = jnp.sum(jnp.where(key >= cand, 1.0, 0.0), axis=1, keepdims=True)
        thr = jnp.where(cnt >= cap, cand, thr)
    gt = key > thr
    eq = key == thr
    n_gt = jnp.sum(jnp.where(gt, 1.0, 0.0), axis=1, keepdims=True)
    need = cap - n_gt
    r = lax.broadcasted_iota(I32, (LANES, LANES), 0)
    cidx = lax.broadcasted_iota(I32, (LANES, LANES), 1)
    tri_b = jnp.where(r <= cidx, 1.0, 0.0).astype(BF16)
    _prefix_excl(jnp.where(eq, 1.0, 0.0), eqr_s, tri_b)
    sel = jnp.where(gt, 1.0, jnp.where(eq, jnp.where(eqr_s[...] < need, 1.0, 0.0), 0.0))
    _prefix_excl(sel, eqr_s, tri_b)
    cnt = eqr_s[...].astype(I32)
    cnt_ref[...] = cnt
    selrank_ref[...] = jnp.where(sel > 0.5, cnt, -1)


def select_topk(aff_t, cap):
    e, n = aff_t.shape
    return pl.pallas_call(
        functools.partial(_select_kernel, cap=cap),
        out_shape=[jax.ShapeDtypeStruct((e, n), I32), jax.ShapeDtypeStruct((e, n), I32)],
        scratch_shapes=[pltpu.VMEM((e, n), F32)],
        compiler_params=pltpu.CompilerParams(vmem_limit_bytes=VMEM_LIMIT),
        name="select_topk",
    )(aff_t)


def _compact_kernel(sr_ref, aff_ref, idx_ref, gate_ref, acc_s, *, cap, tc, rt):
    n = sr_ref.shape[-1]
    acc_s[...] = jnp.zeros_like(acc_s)
    rows8 = lax.broadcasted_iota(I32, (8, tc), 0)
    lane8 = lax.broadcasted_iota(I32, (8, tc), 1)
    r_iota = lax.broadcasted_iota(I32, (rt, tc), 0)

    def body(it, carry):
        ci = it // (cap // rt)
        ri = it % (cap // rt)
        c0 = pl.multiple_of(ci * tc, tc)
        q0 = pl.multiple_of(ri * rt, rt)
        sr = sr_ref[0, :, pl.ds(c0, tc)]
        a = aff_ref[0, :, pl.ds(c0, tc)]
        onehot = jnp.where(sr == r_iota + q0, 1.0, 0.0).astype(BF16)
        t = lane8 + c0
        a_hi = a.astype(BF16)
        res = a - a_hi.astype(F32)
        a_mid = res.astype(BF16)
        a_lo = (res - a_mid.astype(F32)).astype(BF16)
        lhs = jnp.where(rows8 == 0, jnp.right_shift(t, 7).astype(F32),
              jnp.where(rows8 == 1, jnp.bitwise_and(t, LANES - 1).astype(F32),
              jnp.where(rows8 == 2, a_hi.astype(F32),
              jnp.where(rows8 == 3, a_mid.astype(F32),
              jnp.where(rows8 == 4, a_lo.astype(F32), 0.0))))).astype(BF16)
        part = lax.dot_general(lhs, onehot, NT_DIMS, preferred_element_type=F32)
        acc_s[:, pl.ds(q0, rt)] += part
        return carry

    lax.fori_loop(0, (n // tc) * (cap // rt), body, 0)
    acc = acc_s[...]
    idx_ref[0] = (acc[0:1] * float(LANES) + acc[1:2]).astype(I32)
    gate_ref[0] = acc[2:3] + acc[3:4] + acc[4:5]


def compact(selrank, aff_t, cap, tc=1024, rt=256):
    e, n = selrank.shape
    tc = min(tc, n)
    rt = min(rt, cap)
    sr3 = selrank.reshape(e, 1, n)
    af3 = aff_t.reshape(e, 1, n)
    idx, gate = pl.pallas_call(
        functools.partial(_compact_kernel, cap=cap, tc=tc, rt=rt),
        grid=(e,),
        in_specs=[pl.BlockSpec((1, 1, n), lambda i: (i, 0, 0)),
                  pl.BlockSpec((1, 1, n), lambda i: (i, 0, 0))],
        out_specs=[pl.BlockSpec((1, 1, cap), lambda i: (i, 0, 0)),
                   pl.BlockSpec((1, 1, cap), lambda i: (i, 0, 0))],
        out_shape=[jax.ShapeDtypeStruct((e, 1, cap), I32),
                   jax.ShapeDtypeStruct((e, 1, cap), F32)],
        scratch_shapes=[pltpu.VMEM((8, cap), F32)],
        compiler_params=_params("parallel"),
        name="compact",
    )(sr3, af3)
    return idx.reshape(e, cap), gate.reshape(e, cap)


def _ffn_kernel(idx_ref, hn_hbm, gate_ref, w1_ref, w3_ref, w2_ref, y_ref,
                xf_s, xb_s, hid_s, sem, *, cap, nf, bf, gather_rows):
    e = pl.program_id(0)
    s = pl.program_id(1)

    def row_copy(row, slot):
        return pltpu.make_async_copy(hn_hbm.at[pl.ds(row, 1), :],
                                     xf_s.at[pl.ds(slot, 1), :], sem)

    @pl.when(s == 0)
    def _():
        for part in range(cap // gather_rows):
            def issue(r, carry, part=part):
                row_copy(idx_ref[e * cap + part * gather_rows + r], r).start()
                return carry

            lax.fori_loop(0, gather_rows, issue, 0)

            def drain(r, carry):
                row_copy(0, r).wait()
                return carry

            lax.fori_loop(0, gather_rows, drain, 0)
            xb_s[part * gather_rows:(part + 1) * gather_rows, :] = xf_s[...].astype(BF16)

    @pl.when(s < nf)
    def _():
        x = xb_s[...]
        a = jnp.dot(x, w1_ref[0].astype(BF16), preferred_element_type=F32)
        b = jnp.dot(x, w3_ref[0].astype(BF16), preferred_element_type=F32)
        col = pl.multiple_of(s * bf, bf)
        hid_s[:, pl.ds(col, bf)] = (a * jax.nn.sigmoid(a) * b).astype(BF16)

    @pl.when(s >= nf)
    def _():
        y = jnp.dot(hid_s[...], w2_ref[0].astype(BF16), preferred_element_type=F32)
        y_ref[0] = (y * gate_ref[0]).astype(y_ref.dtype)


def expert_ffn(idx, gate, hn, w1, w3, w2, bf=256, bn=512, gather_rows=256):
    e, cap = idx.shape
    d = hn.shape[1]
    f = w1.shape[2]
    bf = min(bf, f)
    bn = min(bn, d)
    gather_rows = min(gather_rows, cap)
    nf = f // bf
    nn = d // bn
    grid_spec = pltpu.PrefetchScalarGridSpec(
        num_scalar_prefetch=1,
        grid=(e, nf + nn),
        in_specs=[pl.BlockSpec(memory_space=pl.ANY),
                  pl.BlockSpec((1, cap, 1), lambda i, s, idx: (i, 0, 0)),
                  pl.BlockSpec((1, d, bf), lambda i, s, idx: (i, 0, jnp.minimum(s, nf - 1))),
                  pl.BlockSpec((1, d, bf), lambda i, s, idx: (i, 0, jnp.minimum(s, nf - 1))),
                  pl.BlockSpec((1, f, bn), lambda i, s, idx: (i, 0, jnp.maximum(s - nf, 0)))],
        out_specs=pl.BlockSpec((1, cap, bn), lambda i, s, idx: (i, 0, jnp.maximum(s - nf, 0))),
        scratch_shapes=[pltpu.VMEM((gather_rows, d), F32),
                        pltpu.VMEM((cap, d), BF16),
                        pltpu.VMEM((cap, f), BF16),
                        pltpu.SemaphoreType.DMA(())],
    )
    return pl.pallas_call(
        functools.partial(_ffn_kernel, cap=cap, nf=nf, bf=bf, gather_rows=gather_rows),
        grid_spec=grid_spec,
        out_shape=jax.ShapeDtypeStruct((e, cap, d), BF16),
        compiler_params=_params("arbitrary", "arbitrary"),
        name="expert_ffn",
    )(idx.reshape(-1), hn, gate.reshape(e, cap, 1), w1, w3, w2)


def _combine_kernel(lo_ref, h_ref, sr_ref, y_hbm, nw_ref, o_ref, acc_s, buf_s, sem,
                    *, n_exp, n_tiles, cap, win):
    i = pl.program_id(0)
    tt = h_ref.shape[0]
    acc_s[...] = h_ref[...]
    j_iota = lax.broadcasted_iota(I32, (tt, win), 1)

    for e in range(n_exp):
        lo = lo_ref[e * (n_tiles + 1) + i]
        hi = lo_ref[e * (n_tiles + 1) + i + 1]
        start0 = jnp.minimum((lo // BF16_SUBLANES) * BF16_SUBLANES, cap - win)
        sr = sr_ref[:, e:e + 1]

        def window(start, first_slot, e=e, sr=sr):
            start = pl.multiple_of(start, BF16_SUBLANES)
            cp = pltpu.make_async_copy(y_hbm.at[e, pl.ds(start, win), :], buf_s, sem)
            cp.start()
            cp.wait()
            slot = j_iota + start
            hit = jnp.where(sr == slot, jnp.where(slot >= first_slot, 1.0, 0.0), 0.0)
            acc_s[...] += jnp.dot(hit.astype(BF16), buf_s[...], preferred_element_type=F32)

        @pl.when(hi > lo)
        def _():
            window(start0, lo)

        @pl.when(hi > start0 + win)
        def _():
            window(jnp.minimum(start0 + win, cap - win), start0 + win)

    x = acc_s[...]
    ms = jnp.mean(x * x, axis=-1, keepdims=True)
    o_ref[...] = x * lax.rsqrt(ms + EPS) * nw_ref[...]


def combine(h, selrank_t, lo_tab, y, norm_w, tt=256):
    n, d = h.shape
    e, cap, _ = y.shape
    tt = min(tt, cap, n)
    n_tiles = n // tt
    grid_spec = pltpu.PrefetchScalarGridSpec(
        num_scalar_prefetch=1,
        grid=(n_tiles,),
        in_specs=[pl.BlockSpec((tt, d), lambda i, lo: (i, 0)),
                  pl.BlockSpec((tt, e), lambda i, lo: (i, 0)),
                  pl.BlockSpec(memory_space=pl.ANY),
                  pl.BlockSpec((1, d), lambda i, lo: (0, 0))],
        out_specs=pl.BlockSpec((tt, d), lambda i, lo: (i, 0)),
        scratch_shapes=[pltpu.VMEM((tt, d), F32),
                        pltpu.VMEM((tt, d), BF16),
                        pltpu.SemaphoreType.DMA(())],
    )
    return pl.pallas_call(
        functools.partial(_combine_kernel, n_exp=e, n_tiles=n_tiles, cap=cap, win=tt),
        grid_spec=grid_spec,
        out_shape=jax.ShapeDtypeStruct((n, d), F32),
        compiler_params=_params("arbitrary"),
        name="combine",
    )(lo_tab.reshape(-1), h, selrank_t, y, norm_w.reshape(1, d))


def moe_block(h, norm_w, w_router, w1, w3, w2, norm_final_w):
    n, _ = h.shape
    e = w_router.shape[1]
    cap = CAPACITY_FACTOR * n // e
    hn, aff_t = norm_router(h, norm_w, w_router.T)
    selrank, cnt = select_topk(aff_t, cap)
    idx, gate = compact(selrank, aff_t, cap)
    y = expert_ffn(idx, gate, hn, w1, w3, w2)
    tt = min(256, cap, n)
    lo_tab = jnp.concatenate([cnt[:, ::tt], jnp.full((e, 1), cap, I32)], axis=1)
    return combine(h, selrank.T, lo_tab, y, norm_final_w, tt=tt)


def mixer_block(x, norm_w, w_in, up_f, bias_f, up_b, bias_b, gla_norm_w,
                lb_logits_f, lb_logits_b, hgrn_norm_w, w_out):
    d = x.shape[1]
    gk = GLA_HEADS * GLA_DK
    gv = GLA_HEADS * GLA_DV
    hd = HGRN_HEADS * HGRN_D
    r = GLA_GATE_RANK
    o_q, o_k, o_v = 0, gk, 2 * gk
    o_low = 2 * gk + gv
    o_og = o_low + 2 * r
    o_hq = o_og + gv
    w_main = jnp.concatenate([w_in[:, :o_low], w_in[:, o_og:]], axis=1).astype(BF16)
    w_low = jnp.pad(w_in[:, o_low:o_og], ((0, 0), (0, LANES - 2 * r))).astype(BF16)
    cols_gla = {"q": o_q, "k": o_k, "v": o_v, "og": o_low}
    c_hq = o_low + gv
    cols_hgrn = {"q": c_hq, "f_fwd": c_hq + hd, "f_bwd": c_hq + 2 * hd,
                 "v": c_hq + 3 * hd, "og": c_hq + 4 * hd}
    del o_hq

    xn = rmsnorm(x, norm_w, BF16)
    proj = matmul(xn, w_main, 1024, 1024, "inproj")
    low = matmul(xn, w_low, 1024, LANES, "inproj_gate")

    u_f = jnp.zeros((LANES, gk), F32).at[0:r].set(up_f).astype(BF16)
    u_b = jnp.zeros((LANES, gk), F32).at[r:2 * r].set(up_b).astype(BF16)
    gla_f = scan("gla", False, proj, {"low": low, "u": u_f, "bias": bias_f.reshape(1, gk)},
                 GLA_HEADS, GLA_DK, GLA_DV, cols_gla)
    gla_o = scan("gla", True, proj,
                 {"low": low, "u": u_b, "bias": bias_b.reshape(1, gk), "prev": gla_f,
                  "norm_w": gla_norm_w.reshape(1, GLA_DV)},
                 GLA_HEADS, GLA_DK, GLA_DV, cols_gla)
    ch = dict(cols_hgrn, f=cols_hgrn["f_fwd"])
    hg_f = scan("hgrn", False, proj, {"lb_logits": lb_logits_f},
                HGRN_HEADS, HGRN_D, HGRN_D, ch)
    ch = dict(cols_hgrn, f=cols_hgrn["f_bwd"])
    hg_o = scan("hgrn", True, proj,
                {"lb_logits": lb_logits_b, "prev": hg_f,
                 "norm_w": hgrn_norm_w.reshape(1, HGRN_D)},
                HGRN_HEADS, HGRN_D, HGRN_D, ch)
    return outproj(gla_o, hg_o, w_out.astype(BF16), x)


def kernel(x, norm_mix_w, w_in, gla_gate_up_f, gla_gate_bias_f, gla_gate_up_b, gla_gate_bias_b, gla_norm_w, hgrn_lb_logits_f, hgrn_lb_logits_b, hgrn_norm_w, w_out, norm_ffn_w, w_router, expert_w1, expert_w3, expert_w2, norm_final_w):
    b, l, d = x.shape
    outs = []
    for bi in range(b):
        xb = x[bi]
        h = mixer_block(xb, norm_mix_w[0], w_in[0], gla_gate_up_f[0], gla_gate_bias_f[0],
                        gla_gate_up_b[0], gla_gate_bias_b[0], gla_norm_w[0],
                        hgrn_lb_logits_f, hgrn_lb_logits_b, hgrn_norm_w[0], w_out[0])
        outs.append(moe_block(h, norm_ffn_w[0], w_router[0], expert_w1[0], expert_w3[0],
                              expert_w2[0], norm_final_w))
    return jnp.stack(outs, axis=0)
```

```python
import functools

import jax
import jax.numpy as jnp
from jax import lax
from jax.experimental import pallas as pl
from jax.experimental.pallas import tpu as pltpu

F32 = jnp.float32
BF16 = jnp.bfloat16
I32 = jnp.int32

EPS = 1e-6
LANES = 128
BF16_SUBLANES = 16
VMEM_LIMIT = 56 * 1024 * 1024

GLA_HEADS = 8
GLA_DK = 128
GLA_DV = 256
GLA_GATE_RANK = 16
GLA_GATE_NORMALIZER = 16.0
HGRN_HEADS = 16
HGRN_D = 128
CHUNK = 64
SUB = 16
N_EXPERTS = 16
CAPACITY_FACTOR = 2

NT_DIMS = (((1,), (1,)), ((), ()))
TN_DIMS = (((0,), (0,)), ((), ()))


def _params(*sem):
    return pltpu.CompilerParams(dimension_semantics=sem, vmem_limit_bytes=VMEM_LIMIT)


def _rmsnorm_kernel(x_ref, w_ref, o_ref):
    x = x_ref[...]
    ms = jnp.mean(x * x, axis=-1, keepdims=True)
    o_ref[...] = (x * lax.rsqrt(ms + EPS) * w_ref[...]).astype(o_ref.dtype)


def rmsnorm(x, w, out_dtype, tm=256):
    n, d = x.shape
    return pl.pallas_call(
        _rmsnorm_kernel,
        grid=(n // tm,),
        in_specs=[pl.BlockSpec((tm, d), lambda i: (i, 0)),
                  pl.BlockSpec((1, d), lambda i: (0, 0))],
        out_specs=pl.BlockSpec((tm, d), lambda i: (i, 0)),
        out_shape=jax.ShapeDtypeStruct((n, d), out_dtype),
        compiler_params=_params("parallel"),
        name="rmsnorm",
    )(x, w.reshape(1, d))


def _mm_kernel(a_ref, b_ref, o_ref):
    o_ref[...] = jnp.dot(a_ref[...], b_ref[...], preferred_element_type=F32)


def matmul(a, b, bm, bn, name):
    m, k = a.shape
    _, n = b.shape
    bm, bn = min(bm, m), min(bn, n)
    return pl.pallas_call(
        _mm_kernel,
        grid=(m // bm, n // bn),
        in_specs=[pl.BlockSpec((bm, k), lambda i, j: (i, 0)),
                  pl.BlockSpec((k, bn), lambda i, j: (0, j))],
        out_specs=pl.BlockSpec((bm, bn), lambda i, j: (i, j)),
        out_shape=jax.ShapeDtypeStruct((m, n), F32),
        compiler_params=_params("parallel", "parallel"),
        name=name,
    )(a, b)


def _scan_kernel(*refs, kind, rev, tb, dk, dv):
    if kind == "gla":
        q_ref, k_ref, v_ref, low_ref, u_ref, b_ref = refs[:6]
        rest = refs[6:]
    else:
        q_ref, f_ref, v_ref, lbl_ref = refs[:4]
        rest = refs[4:]
    if rev:
        prev_ref, og_ref, nw_ref, o_ref = rest[:4]
        scr = rest[4:]
    else:
        o_ref = rest[0]
        scr = rest[1:]
    st_ref, q_s, k_s, cum_s, o_s = scr

    @pl.when(pl.program_id(1) == 0)
    def _():
        st_ref[...] = jnp.zeros_like(st_ref)

    scale = dk ** -0.5
    if kind == "gla":
        z = jnp.dot(low_ref[...].astype(BF16), u_ref[...],
                    preferred_element_type=F32) + b_ref[...]
        log_sig = jnp.minimum(z, 0.0) - jnp.log1p(jnp.exp(-jnp.abs(z)))
        g = log_sig * (1.0 / GLA_GATE_NORMALIZER)
        q_s[...] = q_ref[...] * scale
        k_s[...] = k_ref[...]
    else:
        logits = lbl_ref[...]
        e = jnp.exp(logits - jnp.max(logits, axis=0, keepdims=True))
        lb = e[0:1] / jnp.sum(e, axis=0, keepdims=True)
        f = f_ref[...]
        k_s[...] = (1.0 - lb) * jax.nn.sigmoid(-f)
        g = jnp.log(lb + (1.0 - lb) * jax.nn.sigmoid(f))
        qq = q_ref[...]
        q_s[...] = qq * jax.nn.sigmoid(qq) * scale

    c = CHUNK
    n_sub = c // SUB
    row_cc = lax.broadcasted_iota(I32, (c, c), 0)
    col_cc = lax.broadcasted_iota(I32, (c, c), 1)
    tri = (col_cc >= row_cc) if rev else (col_cc <= row_cc)
    tri_b = jnp.where(tri, 1.0, 0.0).astype(BF16)
    row_ck = lax.broadcasted_iota(I32, (c, dk), 0)
    lane_sc = lax.broadcasted_iota(I32, (SUB, c), 1)
    t_loc = lax.broadcasted_iota(I32, (SUB, 1), 0)

    n_chunk = tb // c
    order = range(n_chunk - 1, -1, -1) if rev else range(n_chunk)
    for j in order:
        r0 = j * c
        gc = g[r0:r0 + c]
        g_hi = gc.astype(BF16)
        res = gc - g_hi.astype(F32)
        g_mid = res.astype(BF16)
        g_lo = (res - g_mid.astype(F32)).astype(BF16)
        cum = (jnp.dot(tri_b, g_hi, preferred_element_type=F32)
               + jnp.dot(tri_b, g_mid, preferred_element_type=F32)
               + jnp.dot(tri_b, g_lo, preferred_element_type=F32))
        cum_s[r0:r0 + c, :] = cum
        qc = q_s[r0:r0 + c]
        kc = k_s[r0:r0 + c]
        vb = v_ref[r0:r0 + c].astype(BF16)
        tot = cum[0:1] if rev else cum[c - 1:c]

        st = st_ref[...]
        qh = (qc * jnp.exp(cum)).astype(BF16)
        o_inter = lax.dot_general(qh, st.astype(BF16), NT_DIMS,
                                  preferred_element_type=F32)
        kdec = (kc * jnp.exp(tot - cum)).astype(BF16)
        st_ref[...] = jnp.exp(tot) * st + lax.dot_general(
            vb, kdec, TN_DIMS, preferred_element_type=F32)

        blocks = []
        for j1 in range(n_sub):
            lo, hi = j1 * SUB, (j1 + 1) * SUB
            if (not rev and j1 == 0) or (rev and j1 == n_sub - 1):
                blocks.append(jnp.zeros((SUB, c), F32))
                continue
            if rev:
                ref_row = cum[hi:hi + 1]
                valid = row_ck >= hi
            else:
                ref_row = cum[lo - 1:lo]
                valid = row_ck < lo
            qt = (qc[lo:hi] * jnp.exp(cum[lo:hi] - ref_row)).astype(BF16)
            kt = (kc * jnp.exp(jnp.where(valid, ref_row - cum, -jnp.inf))).astype(BF16)
            blocks.append(lax.dot_general(qt, kt, NT_DIMS, preferred_element_type=F32))
        for j1 in range(n_sub):
            lo = j1 * SUB
            qb = qc[lo:lo + SUB]
            cb = cum[lo:lo + SUB]
            blk = blocks[j1]
            for sl in range(SUB):
                s = lo + sl
                cs = cum_s[r0 + s:r0 + s + 1, :]
                ks = k_s[r0 + s:r0 + s + 1, :]
                valid_t = (t_loc <= sl) if rev else (t_loc >= sl)
                p = qb * ks * jnp.exp(jnp.where(valid_t, cb - cs, -jnp.inf))
                col = jnp.sum(p, axis=1, keepdims=True)
                blk = jnp.where(lane_sc == s, col, blk)
            blocks[j1] = blk
        a = jnp.concatenate(blocks, axis=0).astype(BF16)
        o_intra = jnp.dot(a, vb, preferred_element_type=F32)
        o_s[r0:r0 + c, :] = o_inter + o_intra

    o = o_s[...]
    if rev:
        t = prev_ref[...] + o
        ms = jnp.mean(t * t, axis=-1, keepdims=True)
        y = t * lax.rsqrt(ms + EPS) * nw_ref[...]
        og = og_ref[...]
        o_ref[...] = (y * (og * jax.nn.sigmoid(og))).astype(o_ref.dtype)
    else:
        o_ref[...] = o


def scan(kind, rev, proj, extras, heads, dk, dv, cols, tb=256):
    n = proj.shape[0]
    nb = n // tb

    def blk(c):
        return (nb - 1 - c) if rev else c

    def col_spec(width, off):
        base = off // width
        return pl.BlockSpec((tb, width), lambda h, c: (blk(c), base + h))

    if kind == "gla":
        low, u, bias = extras["low"], extras["u"], extras["bias"]
        operands = [proj, proj, proj, low, u, bias]
        in_specs = [col_spec(dk, cols["q"]), col_spec(dk, cols["k"]), col_spec(dv, cols["v"]),
                    pl.BlockSpec((tb, LANES), lambda h, c: (blk(c), 0)),
                    pl.BlockSpec((LANES, dk), lambda h, c: (0, h)),
                    pl.BlockSpec((1, dk), lambda h, c: (0, h))]
    else:
        lbl = extras["lb_logits"]
        operands = [proj, proj, proj, lbl]
        in_specs = [col_spec(dk, cols["q"]), col_spec(dk, cols["f"]), col_spec(dv, cols["v"]),
                    pl.BlockSpec((lbl.shape[0], dk), lambda h, c: (0, h))]
    if rev:
        operands += [extras["prev"], proj, extras["norm_w"]]
        in_specs += [pl.BlockSpec((tb, dv), lambda h, c: (blk(c), h)),
                     col_spec(dv, cols["og"]),
                     pl.BlockSpec((1, dv), lambda h, c: (0, 0))]
    out_dtype = BF16 if rev else F32
    kern = functools.partial(_scan_kernel, kind=kind, rev=rev, tb=tb, dk=dk, dv=dv)
    return pl.pallas_call(
        kern,
        grid=(heads, nb),
        in_specs=in_specs,
        out_specs=pl.BlockSpec((tb, dv), lambda h, c: (blk(c), h)),
        out_shape=jax.ShapeDtypeStruct((n, heads * dv), out_dtype),
        scratch_shapes=[pltpu.VMEM((dv, dk), F32),
                        pltpu.VMEM((tb, dk), F32),
                        pltpu.VMEM((tb, dk), F32),
                        pltpu.VMEM((tb, dk), F32),
                        pltpu.VMEM((tb, dv), F32)],
        compiler_params=_params("parallel", "arbitrary"),
        name=f"scan_{kind}_{'bwd' if rev else 'fwd'}",
    )(*operands)


def _outproj_kernel(a_ref, b_ref, wa_ref, wb_ref, x_ref, o_ref):
    acc = jnp.dot(a_ref[...], wa_ref[...], preferred_element_type=F32)
    acc += jnp.dot(b_ref[...], wb_ref[...], preferred_element_type=F32)
    o_ref[...] = x_ref[...] + acc


def outproj(a, b, w, x, bm=512, bn=1024):
    m, ka = a.shape
    kb = b.shape[1]
    n = w.shape[1]
    assert ka == kb
    bm, bn = min(bm, m), min(bn, n)
    return pl.pallas_call(
        _outproj_kernel,
        grid=(n // bn, m // bm),
        in_specs=[pl.BlockSpec((bm, ka), lambda j, i: (i, 0)),
                  pl.BlockSpec((bm, kb), lambda j, i: (i, 0)),
                  pl.BlockSpec((ka, bn), lambda j, i: (0, j)),
                  pl.BlockSpec((kb, bn), lambda j, i: (1, j)),
                  pl.BlockSpec((bm, bn), lambda j, i: (i, j))],
        out_specs=pl.BlockSpec((bm, bn), lambda j, i: (i, j)),
        out_shape=jax.ShapeDtypeStruct((m, n), F32),
        compiler_params=_params("parallel", "parallel"),
        name="outproj",
    )(a, b, w, w, x)


def _norm_router_kernel(h_ref, w_ref, wr_ref, hn_ref, aff_ref):
    x = h_ref[...]
    ms = jnp.mean(x * x, axis=-1, keepdims=True)
    hn = x * lax.rsqrt(ms + EPS) * w_ref[...]
    hn_ref[...] = hn
    logits = lax.dot_general(wr_ref[...], hn, NT_DIMS, preferred_element_type=F32,
                             precision=lax.Precision.HIGHEST)
    m = jnp.max(logits, axis=0, keepdims=True)
    p = jnp.exp(logits - m)
    aff_ref[...] = p / jnp.sum(p, axis=0, keepdims=True)


def norm_router(h, w, w_router_t, tm=256):
    n, d = h.shape
    e = w_router_t.shape[0]
    return pl.pallas_call(
        _norm_router_kernel,
        grid=(n // tm,),
        in_specs=[pl.BlockSpec((tm, d), lambda i: (i, 0)),
                  pl.BlockSpec((1, d), lambda i: (0, 0)),
                  pl.BlockSpec((e, d), lambda i: (0, 0))],
        out_specs=[pl.BlockSpec((tm, d), lambda i: (i, 0)),
                   pl.BlockSpec((e, tm), lambda i: (0, i))],
        out_shape=[jax.ShapeDtypeStruct((n, d), F32),
                   jax.ShapeDtypeStruct((e, n), F32)],
        compiler_params=_params("parallel"),
        name="norm_router",
    )(h, w.reshape(1, d), w_router_t)


def _prefix_excl(m_f, out_ref, tri_b):
    e, n = m_f.shape
    off = jnp.zeros((e, 1), F32)
    for b in range(n // LANES):
        mb = m_f[:, b * LANES:(b + 1) * LANES]
        inc = jnp.dot(mb.astype(BF16), tri_b, preferred_element_type=F32)
        out_ref[:, b * LANES:(b + 1) * LANES] = inc - mb + off
        off = off + inc[:, LANES - 1:LANES]


def _select_kernel(aff_ref, selrank_ref, cnt_ref, eqr_s, *, cap):
    a = aff_ref[...]
    e, n = a.shape
    bits = jnp.zeros((e, 1), I32)
    for b in range(30, -1, -1):
        cand = bits | (1 << b)
        cnt = jnp.sum(jnp.where(a >= pltpu.bitcast(cand, F32), 1.0, 0.0), axis=1, keepdims=True)
        bits = jnp.where(cnt >= cap, cand, bits)
    thr = pltpu.bitcast(bits, F32)
    gt = a > thr
    eq = a == thr
    n_gt = jnp.sum(jnp.where(gt, 1.0, 0.0), axis=1, keepdims=True)
    need = cap - n_gt
    r = lax.broadcasted_iota(I32, (LANES, LANES), 0)
    cidx = lax.broadcasted_iota(I32, (LANES, LANES), 1)
    tri_b = jnp.where(r <= cidx, 1.0, 0.0).astype(BF16)
    _prefix_excl(jnp.where(eq, 1.0, 0.0), eqr_s, tri_b)
    sel = jnp.where(gt, 1.0, jnp.where(eq, jnp.where(eqr_s[...] < need, 1.0, 0.0), 0.0))
    _prefix_excl(sel, eqr_s, tri_b)
    cnt = eqr_s[...].astype(I32)
    cnt_ref[...] = cnt
    selrank_ref[...] = jnp.where(sel > 0.5, cnt, -1)


def select_topk(aff_t, cap):
    e, n = aff_t.shape
    return pl.pallas_call(
        functools.partial(_select_kernel, cap=cap),
        out_shape=[jax.ShapeDtypeStruct((e, n), I32), jax.ShapeDtypeStruct((e, n), I32)],
        scratch_shapes=[pltpu.VMEM((e, n), F32)],
        compiler_params=pltpu.CompilerParams(vmem_limit_bytes=VMEM_LIMIT),
        name="select_topk",
    )(aff_t)


def _compact_kernel(sr_ref, aff_ref, idx_ref, gate_ref, acc_s, *, cap, tc, rt):
    n = sr_ref.shape[-1]
    acc_s[...] = jnp.zeros_like(acc_s)
    rows8 = lax.broadcasted_iota(I32, (8, tc), 0)
    lane8 = lax.broadcasted_iota(I32, (8, tc), 1)
    r_iota = lax.broadcasted_iota(I32, (rt, tc), 0)

    def body(it, carry):
        ci = it // (cap // rt)
        ri = it % (cap // rt)
        c0 = pl.multiple_of(ci * tc, tc)
        q0 = pl.multiple_of(ri * rt, rt)
        sr = sr_ref[0, :, pl.ds(c0, tc)]
        a = aff_ref[0, :, pl.ds(c0, tc)]
        onehot = jnp.where(sr == r_iota + q0, 1.0, 0.0).astype(BF16)
        t = lane8 + c0
        a_hi = a.astype(BF16)
        res = a - a_hi.astype(F32)
        a_mid = res.astype(BF16)
        a_lo = (res - a_mid.astype(F32)).astype(BF16)
        lhs = jnp.where(rows8 == 0, jnp.right_shift(t, 7).astype(F32),
              jnp.where(rows8 == 1, jnp.bitwise_and(t, LANES - 1).astype(F32),
              jnp.where(rows8 == 2, a_hi.astype(F32),
              jnp.where(rows8 == 3, a_mid.astype(F32),
              jnp.where(rows8 == 4, a_lo.astype(F32), 0.0))))).astype(BF16)
        part = lax.dot_general(lhs, onehot, NT_DIMS, preferred_element_type=F32)
        acc_s[:, pl.ds(q0, rt)] += part
        return carry

    lax.fori_loop(0, (n // tc) * (cap // rt), body, 0)
    acc = acc_s[...]
    idx_ref[0] = (acc[0:1] * float(LANES) + acc[1:2]).astype(I32)
    gate_ref[0] = acc[2:3] + acc[3:4] + acc[4:5]


def compact(selrank, aff_t, cap, tc=1024, rt=256):
    e, n = selrank.shape
    tc = min(tc, n)
    rt = min(rt, cap)
    sr3 = selrank.reshape(e, 1, n)
    af3 = aff_t.reshape(e, 1, n)
    idx, gate = pl.pallas_call(
        functools.partial(_compact_kernel, cap=cap, tc=tc, rt=rt),
        grid=(e,),
        in_specs=[pl.BlockSpec((1, 1, n), lambda i: (i, 0, 0)),
                  pl.BlockSpec((1, 1, n), lambda i: (i, 0, 0))],
        out_specs=[pl.BlockSpec((1, 1, cap), lambda i: (i, 0, 0)),
                   pl.BlockSpec((1, 1, cap), lambda i: (i, 0, 0))],
        out_shape=[jax.ShapeDtypeStruct((e, 1, cap), I32),
                   jax.ShapeDtypeStruct((e, 1, cap), F32)],
        scratch_shapes=[pltpu.VMEM((8, cap), F32)],
        compiler_params=_params("parallel"),
        name="compact",
    )(sr3, af3)
    return idx.reshape(e, cap), gate.reshape(e, cap)


def _ffn_kernel(idx_ref, hn_hbm, gate_ref, w1_ref, w3_ref, w2_ref, y_ref,
                xf_s, xb_s, hid_s, sem, *, cap, nf, bf, gather_rows):
    e = pl.program_id(0)
    s = pl.program_id(1)

    def row_copy(row, slot):
        return pltpu.make_async_copy(hn_hbm.at[pl.ds(row, 1), :],
                                     xf_s.at[pl.ds(slot, 1), :], sem)

    @pl.when(s == 0)
    def _():
        for part in range(cap // gather_rows):
            def issue(r, carry, part=part):
                row_copy(idx_ref[e * cap + part * gather_rows + r], r).start()
                return carry

            lax.fori_loop(0, gather_rows, issue, 0)

            def drain(r, carry):
                row_copy(0, r).wait()
                return carry

            lax.fori_loop(0, gather_rows, drain, 0)
            xb_s[part * gather_rows:(part + 1) * gather_rows, :] = xf_s[...].astype(BF16)

    @pl.when(s < nf)
    def _():
        x = xb_s[...]
        a = jnp.dot(x, w1_ref[0].astype(BF16), preferred_element_type=F32)
        b = jnp.dot(x, w3_ref[0].astype(BF16), preferred_element_type=F32)
        col = pl.multiple_of(s * bf, bf)
        hid_s[:, pl.ds(col, bf)] = (a * jax.nn.sigmoid(a) * b).astype(BF16)

    @pl.when(s >= nf)
    def _():
        y = jnp.dot(hid_s[...], w2_ref[0].astype(BF16), preferred_element_type=F32)
        y_ref[0] = (y * gate_ref[0]).astype(y_ref.dtype)


def expert_ffn(idx, gate, hn, w1, w3, w2, bf=256, bn=512, gather_rows=256):
    e, cap = idx.shape
    d = hn.shape[1]
    f = w1.shape[2]
    bf = min(bf, f)
    bn = min(bn, d)
    gather_rows = min(gather_rows, cap)
    nf = f // bf
    nn = d // bn
    grid_spec = pltpu.PrefetchScalarGridSpec(
        num_scalar_prefetch=1,
        grid=(e, nf + nn),
        in_specs=[pl.BlockSpec(memory_space=pl.ANY),
                  pl.BlockSpec((1, cap, 1), lambda i, s, idx: (i, 0, 0)),
                  pl.BlockSpec((1, d, bf), lambda i, s, idx: (i, 0, jnp.minimum(s, nf - 1))),
                  pl.BlockSpec((1, d, bf), lambda i, s, idx: (i, 0, jnp.minimum(s, nf - 1))),
                  pl.BlockSpec((1, f, bn), lambda i, s, idx: (i, 0, jnp.maximum(s - nf, 0)))],
        out_specs=pl.BlockSpec((1, cap, bn), lambda i, s, idx: (i, 0, jnp.maximum(s - nf, 0))),
        scratch_shapes=[pltpu.VMEM((gather_rows, d), F32),
                        pltpu.VMEM((cap, d), BF16),
                        pltpu.VMEM((cap, f), BF16),
                        pltpu.SemaphoreType.DMA(())],
    )
    return pl.pallas_call(
        functools.partial(_ffn_kernel, cap=cap, nf=nf, bf=bf, gather_rows=gather_rows),
        grid_spec=grid_spec,
        out_shape=jax.ShapeDtypeStruct((e, cap, d), BF16),
        compiler_params=_params("arbitrary", "arbitrary"),
        name="expert_ffn",
    )(idx.reshape(-1), hn, gate.reshape(e, cap, 1), w1, w3, w2)


def _combine_kernel(lo_ref, h_ref, sr_ref, y_hbm, nw_ref, o_ref, acc_s, buf_s, sem,
                    *, n_exp, n_tiles, cap, win):
    i = pl.program_id(0)
    tt = h_ref.shape[0]
    acc_s[...] = h_ref[...]
    j_iota = lax.broadcasted_iota(I32, (tt, win), 1)

    for e in range(n_exp):
        lo = lo_ref[e * (n_tiles + 1) + i]
        hi = lo_ref[e * (n_tiles + 1) + i + 1]
        start0 = jnp.minimum((lo // BF16_SUBLANES) * BF16_SUBLANES, cap - win)
        sr = sr_ref[:, e:e + 1]

        def window(start, first_slot, e=e, sr=sr):
            start = pl.multiple_of(start, BF16_SUBLANES)
            cp = pltpu.make_async_copy(y_hbm.at[e, pl.ds(start, win), :], buf_s, sem)
            cp.start()
            cp.wait()
            slot = j_iota + start
            hit = jnp.where(sr == slot, jnp.where(slot >= first_slot, 1.0, 0.0), 0.0)
            acc_s[...] += jnp.dot(hit.astype(BF16), buf_s[...], preferred_element_type=F32)

        @pl.when(hi > lo)
        def _():
            window(start0, lo)

        @pl.when(hi > start0 + win)
        def _():
            window(jnp.minimum(start0 + win, cap - win), start0 + win)

    x = acc_s[...]
    ms = jnp.mean(x * x, axis=-1, keepdims=True)
    o_ref[...] = x * lax.rsqrt(ms + EPS) * nw_ref[...]


def combine(h, selrank_t, lo_tab, y, norm_w, tt=256):
    n, d = h.shape
    e, cap, _ = y.shape
    tt = min(tt, cap, n)
    n_tiles = n // tt
    grid_spec = pltpu.PrefetchScalarGridSpec(
        num_scalar_prefetch=1,
        grid=(n_tiles,),
        in_specs=[pl.BlockSpec((tt, d), lambda i, lo: (i, 0)),
                  pl.BlockSpec((tt, e), lambda i, lo: (i, 0)),
                  pl.BlockSpec(memory_space=pl.ANY),
                  pl.BlockSpec((1, d), lambda i, lo: (0, 0))],
        out_specs=pl.BlockSpec((tt, d), lambda i, lo: (i, 0)),
        scratch_shapes=[pltpu.VMEM((tt, d), F32),
                        pltpu.VMEM((tt, d), BF16),
                        pltpu.SemaphoreType.DMA(())],
    )
    return pl.pallas_call(
        functools.partial(_combine_kernel, n_exp=e, n_tiles=n_tiles, cap=cap, win=tt),
        grid_spec=grid_spec,
        out_shape=jax.ShapeDtypeStruct((n, d), F32),
        compiler_params=_params("arbitrary"),
        name="combine",
    )(lo_tab.reshape(-1), h, selrank_t, y, norm_w.reshape(1, d))


def moe_block(h, norm_w, w_router, w1, w3, w2, norm_final_w):
    n, _ = h.shape
    e = w_router.shape[1]
    cap = CAPACITY_FACTOR * n // e
    hn, aff_t = norm_router(h, norm_w, w_router.T)
    selrank, cnt = select_topk(aff_t, cap)
    idx, gate = compact(selrank, aff_t, cap)
    y = expert_ffn(idx, gate, hn, w1, w3, w2)
    tt = min(256, cap, n)
    lo_tab = jnp.concatenate([cnt[:, ::tt], jnp.full((e, 1), cap, I32)], axis=1)
    return combine(h, selrank.T, lo_tab, y, norm_final_w, tt=tt)


def mixer_block(x, norm_w, w_in, up_f, bias_f, up_b, bias_b, gla_norm_w,
                lb_logits_f, lb_logits_b, hgrn_norm_w, w_out):
    d = x.shape[1]
    gk = GLA_HEADS * GLA_DK
    gv = GLA_HEADS * GLA_DV
    hd = HGRN_HEADS * HGRN_D
    r = GLA_GATE_RANK
    o_q, o_k, o_v = 0, gk, 2 * gk
    o_low = 2 * gk + gv
    o_og = o_low + 2 * r
    o_hq = o_og + gv
    w_main = jnp.concatenate([w_in[:, :o_low], w_in[:, o_og:]], axis=1).astype(BF16)
    w_low = jnp.pad(w_in[:, o_low:o_og], ((0, 0), (0, LANES - 2 * r))).astype(BF16)
    cols_gla = {"q": o_q, "k": o_k, "v": o_v, "og": o_low}
    c_hq = o_low + gv
    cols_hgrn = {"q": c_hq, "f_fwd": c_hq + hd, "f_bwd": c_hq + 2 * hd,
                 "v": c_hq + 3 * hd, "og": c_hq + 4 * hd}
    del o_hq

    xn = rmsnorm(x, norm_w, BF16)
    proj = matmul(xn, w_main, 1024, 1024, "inproj")
    low = matmul(xn, w_low, 1024, LANES, "inproj_gate")

    u_f = jnp.zeros((LANES, gk), F32).at[0:r].set(up_f).astype(BF16)
    u_b = jnp.zeros((LANES, gk), F32).at[r:2 * r].set(up_b).astype(BF16)
    gla_f = scan("gla", False, proj, {"low": low, "u": u_f, "bias": bias_f.reshape(1, gk)},
                 GLA_HEADS, GLA_DK, GLA_DV, cols_gla)
    gla_o = scan("gla", True, proj,
                 {"low": low, "u": u_b, "bias": bias_b.reshape(1, gk), "prev": gla_f,
                  "norm_w": gla_norm_w.reshape(1, GLA_DV)},
                 GLA_HEADS, GLA_DK, GLA_DV, cols_gla)
    ch = dict(cols_hgrn, f=cols_hgrn["f_fwd"])
    hg_f = scan("hgrn", False, proj, {"lb_logits": lb_logits_f},
                HGRN_HEADS, HGRN_D, HGRN_D, ch)
    ch = dict(cols_hgrn, f=cols_hgrn["f_bwd"])
    hg_o = scan("hgrn", True, proj,
                {"lb_logits": lb_logits_b, "prev": hg_f,
                 "norm_w": hgrn_norm_w.reshape(1, HGRN_D)},
                HGRN_HEADS, HGRN_D, HGRN_D, ch)
    return outproj(gla_o, hg_o, w_out.astype(BF16), x)


def kernel(x, norm_mix_w, w_in, gla_gate_up_f, gla_gate_bias_f, gla_gate_up_b, gla_gate_bias_b, gla_norm_w, hgrn_lb_logits_f, hgrn_lb_logits_b, hgrn_norm_w, w_out, norm_ffn_w, w_router, expert_w1, expert_w3, expert_w2, norm_final_w):
    b, l, d = x.shape
    outs = []
    for bi in range(b):
        xb = x[bi]
        h = mixer_block(xb, norm_mix_w[0], w_in[0], gla_gate_up_f[0], gla_gate_bias_f[0],
                        gla_gate_up_b[0], gla_gate_bias_b[0], gla_norm_w[0],
                        hgrn_lb_logits_f, hgrn_lb_logits_b, hgrn_norm_w[0], w_out[0])
        outs.append(moe_block(h, norm_ffn_w[0], w_router[0], expert_w1[0], expert_w3[0],
                              expert_w2[0], norm_final_w))
    return jnp.stack(outs, axis=0)
```

```python
import functools

import jax
import jax.numpy as jnp
from jax import lax
from jax.experimental import pallas as pl
from jax.experimental.pallas import tpu as pltpu

F32 = jnp.float32
BF16 = jnp.bfloat16
I32 = jnp.int32

EPS = 1e-6
LANES = 128
BF16_SUBLANES = 16
VMEM_LIMIT = 56 * 1024 * 1024

GLA_HEADS = 8
GLA_DK = 128
GLA_DV = 256
GLA_GATE_RANK = 16
GLA_GATE_NORMALIZER = 16.0
HGRN_HEADS = 16
HGRN_D = 128
CHUNK = 64
SUB = 16
N_EXPERTS = 16
CAPACITY_FACTOR = 2

NT_DIMS = (((1,), (1,)), ((), ()))
TN_DIMS = (((0,), (0,)), ((), ()))


def _params(*sem):
    return pltpu.CompilerParams(dimension_semantics=sem, vmem_limit_bytes=VMEM_LIMIT)


def _rmsnorm_kernel(x_ref, w_ref, o_ref):
    x = x_ref[...]
    ms = jnp.mean(x * x, axis=-1, keepdims=True)
    o_ref[...] = (x * lax.rsqrt(ms + EPS) * w_ref[...]).astype(o_ref.dtype)


def rmsnorm(x, w, out_dtype, tm=256):
    n, d = x.shape
    return pl.pallas_call(
        _rmsnorm_kernel,
        grid=(n // tm,),
        in_specs=[pl.BlockSpec((tm, d), lambda i: (i, 0)),
                  pl.BlockSpec((1, d), lambda i: (0, 0))],
        out_specs=pl.BlockSpec((tm, d), lambda i: (i, 0)),
        out_shape=jax.ShapeDtypeStruct((n, d), out_dtype),
        compiler_params=_params("parallel"),
        name="rmsnorm",
    )(x, w.reshape(1, d))


def _mm_kernel(a_ref, b_ref, o_ref):
    o_ref[...] = jnp.dot(a_ref[...], b_ref[...].astype(BF16), preferred_element_type=F32)


def matmul(a, b, bm, bn, name, n=None):
    m, k = a.shape
    n = b.shape[1] if n is None else n
    bm, bn = min(bm, m), min(bn, n)
    return pl.pallas_call(
        _mm_kernel,
        grid=(m // bm, n // bn),
        in_specs=[pl.BlockSpec((bm, k), lambda i, j: (i, 0)),
                  pl.BlockSpec((k, bn), lambda i, j: (0, j))],
        out_specs=pl.BlockSpec((bm, bn), lambda i, j: (i, j)),
        out_shape=jax.ShapeDtypeStruct((m, n), F32),
        compiler_params=_params("parallel", "parallel"),
        name=name,
    )(a, b)


def _scan_kernel(*refs, kind, rev, tb, dk, dv):
    if kind == "gla":
        q_ref, k_ref, v_ref, low_ref, u_ref, b_ref = refs[:6]
        rest = refs[6:]
    else:
        q_ref, f_ref, v_ref, lbl_ref = refs[:4]
        rest = refs[4:]
    if rev:
        prev_ref, og_ref, nw_ref, o_ref = rest[:4]
        scr = rest[4:]
    else:
        o_ref = rest[0]
        scr = rest[1:]
    st_ref, q_s, k_s, cum_s, o_s = scr

    @pl.when(pl.program_id(1) == 0)
    def _():
        st_ref[...] = jnp.zeros_like(st_ref)

    scale = dk ** -0.5
    if kind == "gla":
        z = jnp.dot(low_ref[...].astype(BF16), u_ref[...],
                    preferred_element_type=F32) + b_ref[...]
        log_sig = jnp.minimum(z, 0.0) - jnp.log1p(jnp.exp(-jnp.abs(z)))
        g = log_sig * (1.0 / GLA_GATE_NORMALIZER)
        q_s[...] = q_ref[...] * scale
        k_s[...] = k_ref[...]
    else:
        logits = lbl_ref[...]
        e = jnp.exp(logits - jnp.max(logits, axis=0, keepdims=True))
        lb = e[0:1] / jnp.sum(e, axis=0, keepdims=True)
        f = f_ref[...]
        k_s[...] = (1.0 - lb) * jax.nn.sigmoid(-f)
        g = jnp.log(lb + (1.0 - lb) * jax.nn.sigmoid(f))
        qq = q_ref[...]
        q_s[...] = qq * jax.nn.sigmoid(qq) * scale

    c = CHUNK
    n_sub = c // SUB
    row_cc = lax.broadcasted_iota(I32, (c, c), 0)
    col_cc = lax.broadcasted_iota(I32, (c, c), 1)
    tri = (col_cc >= row_cc) if rev else (col_cc <= row_cc)
    tri_b = jnp.where(tri, 1.0, 0.0).astype(BF16)
    row_ck = lax.broadcasted_iota(I32, (c, dk), 0)
    lane_sc = lax.broadcasted_iota(I32, (SUB, c), 1)
    t_loc = lax.broadcasted_iota(I32, (SUB, 1), 0)

    n_chunk = tb // c
    order = range(n_chunk - 1, -1, -1) if rev else range(n_chunk)
    for j in order:
        r0 = j * c
        gc = g[r0:r0 + c]
        g_hi = gc.astype(BF16)
        res = gc - g_hi.astype(F32)
        g_mid = res.astype(BF16)
        g_lo = (res - g_mid.astype(F32)).astype(BF16)
        cum = (jnp.dot(tri_b, g_hi, preferred_element_type=F32)
               + jnp.dot(tri_b, g_mid, preferred_element_type=F32)
               + jnp.dot(tri_b, g_lo, preferred_element_type=F32))
        cum_s[r0:r0 + c, :] = cum
        qc = q_s[r0:r0 + c]
        kc = k_s[r0:r0 + c]
        vb = v_ref[r0:r0 + c].astype(BF16)
        tot = cum[0:1] if rev else cum[c - 1:c]

        st = st_ref[...]
        qh = (qc * jnp.exp(cum)).astype(BF16)
        o_inter = lax.dot_general(qh, st.astype(BF16), NT_DIMS,
                                  preferred_element_type=F32)
        kdec = (kc * jnp.exp(tot - cum)).astype(BF16)
        st_ref[...] = jnp.exp(tot) * st + lax.dot_general(
            vb, kdec, TN_DIMS, preferred_element_type=F32)

        blocks = []
        for j1 in range(n_sub):
            lo, hi = j1 * SUB, (j1 + 1) * SUB
            if (not rev and j1 == 0) or (rev and j1 == n_sub - 1):
                blocks.append(jnp.zeros((SUB, c), F32))
                continue
            if rev:
                ref_row = cum[hi:hi + 1]
                valid = row_ck >= hi
            else:
                ref_row = cum[lo - 1:lo]
                valid = row_ck < lo
            qt = (qc[lo:hi] * jnp.exp(cum[lo:hi] - ref_row)).astype(BF16)
            kt = (kc * jnp.exp(jnp.where(valid, ref_row - cum, -jnp.inf))).astype(BF16)
            blocks.append(lax.dot_general(qt, kt, NT_DIMS, preferred_element_type=F32))
        for j1 in range(n_sub):
            lo = j1 * SUB
            qb = qc[lo:lo + SUB]
            cb = cum[lo:lo + SUB]
            blk = blocks[j1]
            for sl in range(SUB):
                s = lo + sl
                cs = cum_s[r0 + s:r0 + s + 1, :]
                ks = k_s[r0 + s:r0 + s + 1, :]
                valid_t = (t_loc <= sl) if rev else (t_loc >= sl)
                p = qb * ks * jnp.exp(jnp.where(valid_t, cb - cs, -jnp.inf))
                col = jnp.sum(p, axis=1, keepdims=True)
                blk = jnp.where(lane_sc == s, col, blk)
            blocks[j1] = blk
        a = jnp.concatenate(blocks, axis=0).astype(BF16)
        o_intra = jnp.dot(a, vb, preferred_element_type=F32)
        o_s[r0:r0 + c, :] = o_inter + o_intra

    o = o_s[...]
    if rev:
        t = prev_ref[...] + o
        ms = jnp.mean(t * t, axis=-1, keepdims=True)
        y = t * lax.rsqrt(ms + EPS) * nw_ref[...]
        og = og_ref[...]
        o_ref[...] = (y * (og * jax.nn.sigmoid(og))).astype(o_ref.dtype)
    else:
        o_ref[...] = o


def scan(kind, rev, srcs, extras, heads, dk, dv, tb=512):
    n = srcs["q"][0].shape[0]
    tb = min(tb, n)
    nb = n // tb

    def blk(c):
        return (nb - 1 - c) if rev else c

    def col_spec(width, name):
        base = srcs[name][1] // width
        return pl.BlockSpec((tb, width), lambda h, c: (blk(c), base + h))

    if kind == "gla":
        low, u, bias = extras["low"], extras["u"], extras["bias"]
        operands = [srcs["q"][0], srcs["k"][0], srcs["v"][0], low, u, bias]
        in_specs = [col_spec(dk, "q"), col_spec(dk, "k"), col_spec(dv, "v"),
                    pl.BlockSpec((tb, LANES), lambda h, c: (blk(c), 0)),
                    pl.BlockSpec((LANES, dk), lambda h, c: (0, h)),
                    pl.BlockSpec((1, dk), lambda h, c: (0, h))]
    else:
        lbl = extras["lb_logits"]
        operands = [srcs["q"][0], srcs["f"][0], srcs["v"][0], lbl]
        in_specs = [col_spec(dk, "q"), col_spec(dk, "f"), col_spec(dv, "v"),
                    pl.BlockSpec((lbl.shape[0], dk), lambda h, c: (0, h))]
    if rev:
        operands += [extras["prev"], srcs["og"][0], extras["norm_w"]]
        in_specs += [pl.BlockSpec((tb, dv), lambda h, c: (blk(c), h)),
                     col_spec(dv, "og"),
                     pl.BlockSpec((1, dv), lambda h, c: (0, 0))]
    out_dtype = BF16 if rev else F32
    kern = functools.partial(_scan_kernel, kind=kind, rev=rev, tb=tb, dk=dk, dv=dv)
    return pl.pallas_call(
        kern,
        grid=(heads, nb),
        in_specs=in_specs,
        out_specs=pl.BlockSpec((tb, dv), lambda h, c: (blk(c), h)),
        out_shape=jax.ShapeDtypeStruct((n, heads * dv), out_dtype),
        scratch_shapes=[pltpu.VMEM((dv, dk), F32),
                        pltpu.VMEM((tb, dk), F32),
                        pltpu.VMEM((tb, dk), F32),
                        pltpu.VMEM((tb, dk), F32),
                        pltpu.VMEM((tb, dv), F32)],
        compiler_params=_params("parallel", "arbitrary"),
        name=f"scan_{kind}_{'bwd' if rev else 'fwd'}",
    )(*operands)


def _outproj_kernel(a_ref, b_ref, wa_ref, wb_ref, x_ref, o_ref):
    acc = jnp.dot(a_ref[...], wa_ref[...], preferred_element_type=F32)
    acc += jnp.dot(b_ref[...], wb_ref[...], preferred_element_type=F32)
    o_ref[...] = x_ref[...] + acc


def outproj(a, b, w, x, bm=512, bn=1024):
    m, ka = a.shape
    kb = b.shape[1]
    n = w.shape[1]
    assert ka == kb
    bm, bn = min(bm, m), min(bn, n)
    return pl.pallas_call(
        _outproj_kernel,
        grid=(n // bn, m // bm),
        in_specs=[pl.BlockSpec((bm, ka), lambda j, i: (i, 0)),
                  pl.BlockSpec((bm, kb), lambda j, i: (i, 0)),
                  pl.BlockSpec((ka, bn), lambda j, i: (0, j)),
                  pl.BlockSpec((kb, bn), lambda j, i: (1, j)),
                  pl.BlockSpec((bm, bn), lambda j, i: (i, j))],
        out_specs=pl.BlockSpec((bm, bn), lambda j, i: (i, j)),
        out_shape=jax.ShapeDtypeStruct((m, n), F32),
        compiler_params=_params("parallel", "parallel"),
        name="outproj",
    )(a, b, w, w, x)


def _norm_router_kernel(h_ref, w_ref, wr_ref, hn_ref, aff_ref):
    x = h_ref[...]
    ms = jnp.mean(x * x, axis=-1, keepdims=True)
    hn = x * lax.rsqrt(ms + EPS) * w_ref[...]
    d2 = hn.shape[1] // 2
    hn_ref[...] = pltpu.pack_elementwise([hn[:, :d2], hn[:, d2:]], packed_dtype=BF16)
    logits = lax.dot_general(wr_ref[...], hn, NT_DIMS, preferred_element_type=F32,
                             precision=lax.Precision.HIGHEST)
    m = jnp.max(logits, axis=0, keepdims=True)
    p = jnp.exp(logits - m)
    aff_ref[...] = p / jnp.sum(p, axis=0, keepdims=True)


def norm_router(h, w, w_router_t, tm=256):
    n, d = h.shape
    e = w_router_t.shape[0]
    return pl.pallas_call(
        _norm_router_kernel,
        grid=(n // tm,),
        in_specs=[pl.BlockSpec((tm, d), lambda i: (i, 0)),
                  pl.BlockSpec((1, d), lambda i: (0, 0)),
                  pl.BlockSpec((e, d), lambda i: (0, 0))],
        out_specs=[pl.BlockSpec((tm, d // 2), lambda i: (i, 0)),
                   pl.BlockSpec((e, tm), lambda i: (0, i))],
        out_shape=[jax.ShapeDtypeStruct((n, d // 2), I32),
                   jax.ShapeDtypeStruct((e, n), F32)],
        compiler_params=_params("parallel"),
        name="norm_router",
    )(h, w.reshape(1, d), w_router_t)


def _prefix_excl(m_f, out_ref, tri_b):
    e, n = m_f.shape
    off = jnp.zeros((e, 1), F32)
    for b in range(n // LANES):
        mb = m_f[:, b * LANES:(b + 1) * LANES]
        inc = jnp.dot(mb.astype(BF16), tri_b, preferred_element_type=F32)
        out_ref[:, b * LANES:(b + 1) * LANES] = inc - mb + off
        off = off + inc[:, LANES - 1:LANES]


def _select_kernel(aff_ref, selrank_ref, cnt_ref, eqr_s, *, cap):
    a = aff_ref[...]
    e, n = a.shape
    bits = jnp.zeros((e, 1), I32)
    for b in range(30, -1, -1):
        cand = bits | (1 << b)
        cnt = jnp.sum(jnp.where(a >= pltpu.bitcast(cand, F32), 1.0, 0.0), axis=1, keepdims=True)
        bits = jnp.where(cnt >= cap, cand, bits)
    thr = pltpu.bitcast(bits, F32)
    gt = a > thr
    eq = a == thr
    n_gt = jnp.sum(jnp.where(gt, 1.0, 0.0), axis=1, keepdims=True)
    need = cap - n_gt
    r = lax.broadcasted_iota(I32, (LANES, LANES), 0)
    cidx = lax.broadcasted_iota(I32, (LANES, LANES), 1)
    tri_b = jnp.where(r <= cidx, 1.0, 0.0).astype(BF16)
    _prefix_excl(jnp.where(eq, 1.0, 0.0), eqr_s, tri_b)
    sel = jnp.where(gt, 1.0, jnp.where(eq, jnp.where(eqr_s[...] < need, 1.0, 0.0), 0.0))
    _prefix_excl(sel, eqr_s, tri_b)
    cnt = eqr_s[...].astype(I32)
    cnt_ref[...] = cnt
    selrank_ref[...] = jnp.where(sel > 0.5, cnt, -1)


def select_topk(aff_t, cap):
    e, n = aff_t.shape
    return pl.pallas_call(
        functools.partial(_select_kernel, cap=cap),
        out_shape=[jax.ShapeDtypeStruct((e, n), I32), jax.ShapeDtypeStruct((e, n), I32)],
        scratch_shapes=[pltpu.VMEM((e, n), F32)],
        compiler_params=pltpu.CompilerParams(vmem_limit_bytes=VMEM_LIMIT),
        name="select_topk",
    )(aff_t)


def _compact_kernel(sr_ref, aff_ref, idx_ref, gate_ref, acc_s, *, cap, tc, rt):
    n = sr_ref.shape[-1]
    acc_s[...] = jnp.zeros_like(acc_s)
    rows8 = lax.broadcasted_iota(I32, (8, tc), 0)
    lane8 = lax.broadcasted_iota(I32, (8, tc), 1)
    r_iota = lax.broadcasted_iota(I32, (rt, tc), 0)

    def body(it, carry):
        ci = it // (cap // rt)
        ri = it % (cap // rt)
        c0 = pl.multiple_of(ci * tc, tc)
        q0 = pl.multiple_of(ri * rt, rt)
        sr = sr_ref[0, :, pl.ds(c0, tc)]
        a = aff_ref[0, :, pl.ds(c0, tc)]
        onehot = jnp.where(sr == r_iota + q0, 1.0, 0.0).astype(BF16)
        t = lane8 + c0
        a_hi = a.astype(BF16)
        res = a - a_hi.astype(F32)
        a_mid = res.astype(BF16)
        a_lo = (res - a_mid.astype(F32)).astype(BF16)
        lhs = jnp.where(rows8 == 0, jnp.right_shift(t, 7).astype(F32),
              jnp.where(rows8 == 1, jnp.bitwise_and(t, LANES - 1).astype(F32),
              jnp.where(rows8 == 2, a_hi.astype(F32),
              jnp.where(rows8 == 3, a_mid.astype(F32),
              jnp.where(rows8 == 4, a_lo.astype(F32), 0.0))))).astype(BF16)
        part = lax.dot_general(lhs, onehot, NT_DIMS, preferred_element_type=F32)
        acc_s[:, pl.ds(q0, rt)] += part
        return carry

    lax.fori_loop(0, (n // tc) * (cap // rt), body, 0)
    acc = acc_s[...]
    idx_ref[0] = (acc[0:1] * float(LANES) + acc[1:2]).astype(I32)
    gate_ref[0] = acc[2:3] + acc[3:4] + acc[4:5]


def compact(selrank, aff_t, cap, tc=1024, rt=256):
    e, n = selrank.shape
    tc = min(tc, n)
    rt = min(rt, cap)
    sr3 = selrank.reshape(e, 1, n)
    af3 = aff_t.reshape(e, 1, n)
    idx, gate = pl.pallas_call(
        functools.partial(_compact_kernel, cap=cap, tc=tc, rt=rt),
        grid=(e,),
        in_specs=[pl.BlockSpec((1, 1, n), lambda i: (i, 0, 0)),
                  pl.BlockSpec((1, 1, n), lambda i: (i, 0, 0))],
        out_specs=[pl.BlockSpec((1, 1, cap), lambda i: (i, 0, 0)),
                   pl.BlockSpec((1, 1, cap), lambda i: (i, 0, 0))],
        out_shape=[jax.ShapeDtypeStruct((e, 1, cap), I32),
                   jax.ShapeDtypeStruct((e, 1, cap), F32)],
        scratch_shapes=[pltpu.VMEM((8, cap), F32)],
        compiler_params=_params("parallel"),
        name="compact",
    )(sr3, af3)
    return idx.reshape(e, cap), gate.reshape(e, cap)


def _ffn_kernel(idx_ref, hp_hbm, gate_ref, w1_ref, w3_ref, w2_ref, y_ref,
                xg_s, xb_s, hid_s, sem, *, cap, nf, bf, n_exp):
    e = pl.program_id(0)
    s = pl.program_id(1)
    d2 = xg_s.shape[1]

    def issue_gather(expert):
        def issue(r, carry):
            row = idx_ref[expert * cap + r]
            pltpu.make_async_copy(hp_hbm.at[pl.ds(row, 1), :],
                                  xg_s.at[pl.ds(r, 1), :], sem).start()
            return carry

        lax.fori_loop(0, cap, issue, 0, unroll=8)

    @pl.when(s == 0)
    def _():
        @pl.when(e == 0)
        def _():
            issue_gather(0)

        pltpu.make_async_copy(hp_hbm.at[pl.ds(0, cap), :], xg_s, sem).wait()
        xg = xg_s[...]
        xb_s[:, :d2] = pltpu.unpack_elementwise(
            xg, index=0, packed_dtype=BF16, unpacked_dtype=F32).astype(BF16)
        xb_s[:, d2:] = pltpu.unpack_elementwise(
            xg, index=1, packed_dtype=BF16, unpacked_dtype=F32).astype(BF16)

        @pl.when(e + 1 < n_exp)
        def _():
            issue_gather(e + 1)

    @pl.when(s < nf)
    def _():
        x = xb_s[...]
        a = jnp.dot(x, w1_ref[0].astype(BF16), preferred_element_type=F32)
        b = jnp.dot(x, w3_ref[0].astype(BF16), preferred_element_type=F32)
        col = pl.multiple_of(s * bf, bf)
        hid_s[:, pl.ds(col, bf)] = (a * jax.nn.sigmoid(a) * b).astype(BF16)

    @pl.when(s >= nf)
    def _():
        y = jnp.dot(hid_s[...], w2_ref[0].astype(BF16), preferred_element_type=F32)
        y_ref[0] = (y * gate_ref[0]).astype(y_ref.dtype)


def expert_ffn(idx, gate, hn_packed, w1, w3, w2, bf=256, bn=512):
    e, cap = idx.shape
    d2 = hn_packed.shape[1]
    d = 2 * d2
    f = w1.shape[2]
    bf = min(bf, f)
    bn = min(bn, d)
    nf = f // bf
    nn = d // bn
    grid_spec = pltpu.PrefetchScalarGridSpec(
        num_scalar_prefetch=1,
        grid=(e, nf + nn),
        in_specs=[pl.BlockSpec(memory_space=pl.ANY),
                  pl.BlockSpec((1, cap, 1), lambda i, s, idx: (i, 0, 0)),
                  pl.BlockSpec((1, d, bf), lambda i, s, idx: (i, 0, jnp.minimum(s, nf - 1))),
                  pl.BlockSpec((1, d, bf), lambda i, s, idx: (i, 0, jnp.minimum(s, nf - 1))),
                  pl.BlockSpec((1, f, bn), lambda i, s, idx: (i, 0, jnp.maximum(s - nf, 0)))],
        out_specs=pl.BlockSpec((1, cap, bn), lambda i, s, idx: (i, 0, jnp.maximum(s - nf, 0))),
        scratch_shapes=[pltpu.VMEM((cap, d2), I32),
                        pltpu.VMEM((cap, d), BF16),
                        pltpu.VMEM((cap, f), BF16),
                        pltpu.SemaphoreType.DMA(())],
    )
    return pl.pallas_call(
        functools.partial(_ffn_kernel, cap=cap, nf=nf, bf=bf, n_exp=e),
        grid_spec=grid_spec,
        out_shape=jax.ShapeDtypeStruct((e, cap, d), BF16),
        compiler_params=_params("arbitrary", "arbitrary"),
        name="expert_ffn",
    )(idx.reshape(-1), hn_packed, gate.reshape(e, cap, 1), w1, w3, w2)


def _combine_kernel(lo_ref, h_ref, sr_ref, y_hbm, nw_ref, o_ref,
                    acc_s, pack_s, big_s, sem_pack, sem_big,
                    *, n_exp, n_tiles, cap, ws):
    i = pl.program_id(0)
    tt = h_ref.shape[0]
    n_pack = n_exp * ws

    def bounds(tile, e):
        lo = lo_ref[e * (n_tiles + 1) + tile]
        hi = lo_ref[e * (n_tiles + 1) + tile + 1]
        start = jnp.minimum((lo // BF16_SUBLANES) * BF16_SUBLANES, cap - ws)
        return lo, hi, start

    def pack_copy(tile, e, slot):
        _, _, start = bounds(tile, e)
        start = pl.multiple_of(start, BF16_SUBLANES)
        return pltpu.make_async_copy(y_hbm.at[e, pl.ds(start, ws), :],
                                     pack_s.at[slot, pl.ds(e * ws, ws), :],
                                     sem_pack.at[slot])

    def issue(tile, slot):
        for e in range(n_exp):
            pack_copy(tile, e, slot).start()

    @pl.when(i == 0)
    def _():
        issue(0, 0)

    @pl.when(i + 1 < n_tiles)
    def _():
        issue(i + 1, (i + 1) % 2)

    slot = i % 2
    for e in range(n_exp):
        pack_copy(i, e, slot).wait()

    lane = lax.broadcasted_iota(I32, (1, n_pack), 1)
    lane_t = lax.broadcasted_iota(I32, (tt, n_pack), 1)
    target = jnp.full((1, n_pack), -2, I32)
    sr_wide = jnp.full((tt, n_pack), -1, I32)
    for e in range(n_exp):
        lo, hi, start = bounds(i, e)
        covered = hi <= start + ws
        first = jnp.where(covered, start, -(cap + n_pack))
        in_e = (lane >= e * ws) & (lane < (e + 1) * ws)
        target = jnp.where(in_e, first + lane - e * ws, target)
        in_e_t = (lane_t >= e * ws) & (lane_t < (e + 1) * ws)
        sr_wide = jnp.where(in_e_t, sr_ref[:, e:e + 1], sr_wide)
    hit = jnp.where(sr_wide == target, 1.0, 0.0).astype(BF16)
    acc_s[...] = h_ref[...] + jnp.dot(hit, pack_s[slot], preferred_element_type=F32)

    wl = big_s.shape[0]
    j_big = lax.broadcasted_iota(I32, (tt, wl), 1)
    for e in range(n_exp):
        lo, hi, start = bounds(i, e)
        start_l = jnp.minimum(start, cap - wl)
        sr = sr_ref[:, e:e + 1]

        def window(begin, first_slot, e=e, sr=sr):
            begin = pl.multiple_of(begin, BF16_SUBLANES)
            cp = pltpu.make_async_copy(y_hbm.at[e, pl.ds(begin, wl), :], big_s, sem_big)
            cp.start()
            cp.wait()
            slot_id = j_big + begin
            hit_l = jnp.where(sr == slot_id, jnp.where(slot_id >= first_slot, 1.0, 0.0), 0.0)
            acc_s[...] += jnp.dot(hit_l.astype(BF16), big_s[...], preferred_element_type=F32)

        @pl.when(hi > start + ws)
        def _():
            window(start_l, lo)

        @pl.when(hi > start_l + wl)
        def _():
            window(jnp.minimum(start_l + wl, cap - wl), start_l + wl)

    x = acc_s[...]
    ms = jnp.mean(x * x, axis=-1, keepdims=True)
    o_ref[...] = x * lax.rsqrt(ms + EPS) * nw_ref[...]


def combine(h, selrank_t, lo_tab, y, norm_w, tt=256, ws=64):
    n, d = h.shape
    e, cap, _ = y.shape
    tt = min(tt, cap, n)
    ws = min(ws, tt)
    n_tiles = n // tt
    grid_spec = pltpu.PrefetchScalarGridSpec(
        num_scalar_prefetch=1,
        grid=(n_tiles,),
        in_specs=[pl.BlockSpec((tt, d), lambda i, lo: (i, 0)),
                  pl.BlockSpec((tt, e), lambda i, lo: (i, 0)),
                  pl.BlockSpec(memory_space=pl.ANY),
                  pl.BlockSpec((1, d), lambda i, lo: (0, 0))],
        out_specs=pl.BlockSpec((tt, d), lambda i, lo: (i, 0)),
        scratch_shapes=[pltpu.VMEM((tt, d), F32),
                        pltpu.VMEM((2, e * ws, d), BF16),
                        pltpu.VMEM((tt, d), BF16),
                        pltpu.SemaphoreType.DMA((2,)),
                        pltpu.SemaphoreType.DMA(())],
    )
    return pl.pallas_call(
        functools.partial(_combine_kernel, n_exp=e, n_tiles=n_tiles, cap=cap, ws=ws),
        grid_spec=grid_spec,
        out_shape=jax.ShapeDtypeStruct((n, d), F32),
        compiler_params=_params("arbitrary"),
        name="combine",
    )(lo_tab.reshape(-1), h, selrank_t, y, norm_w.reshape(1, d))


def moe_block(h, norm_w, w_router, w1, w3, w2, norm_final_w):
    n, _ = h.shape
    e = w_router.shape[1]
    cap = CAPACITY_FACTOR * n // e
    hn_packed, aff_t = norm_router(h, norm_w, w_router.T)
    selrank, cnt = select_topk(aff_t, cap)
    idx, gate = compact(selrank, aff_t, cap)
    y = expert_ffn(idx, gate, hn_packed, w1, w3, w2)
    tt = min(256, cap, n)
    lo_tab = jnp.concatenate([cnt[:, ::tt], jnp.full((e, 1), cap, I32)], axis=1)
    return combine(h, selrank.T, lo_tab, y, norm_final_w, tt=tt)


def mixer_block(x, norm_w, w_in, up_f, bias_f, up_b, bias_b, gla_norm_w,
                lb_logits_f, lb_logits_b, hgrn_norm_w, w_out):
    gk = GLA_HEADS * GLA_DK
    gv = GLA_HEADS * GLA_DV
    hd = HGRN_HEADS * HGRN_D
    r = GLA_GATE_RANK
    n_a = 2 * gk + gv
    o_b = n_a + 2 * r
    w_b = w_in[:, o_b:].astype(BF16)
    w_low = jnp.pad(w_in[:, n_a:o_b], ((0, 0), (0, LANES - 2 * r))).astype(BF16)

    xn = rmsnorm(x, norm_w, BF16)
    proj_a = matmul(xn, w_in, 1024, 512, "inproj_a", n=n_a)
    proj_b = matmul(xn, w_b, 1024, 1024, "inproj_b")
    low = matmul(xn, w_low, 1024, LANES, "inproj_gate")

    src_gla = {"q": (proj_a, 0), "k": (proj_a, gk), "v": (proj_a, 2 * gk), "og": (proj_b, 0)}
    src_hg = {"q": (proj_b, gv), "v": (proj_b, gv + 3 * hd), "og": (proj_b, gv + 4 * hd)}
    u_f = jnp.zeros((LANES, gk), F32).at[0:r].set(up_f).astype(BF16)
    u_b = jnp.zeros((LANES, gk), F32).at[r:2 * r].set(up_b).astype(BF16)
    gla_f = scan("gla", False, src_gla, {"low": low, "u": u_f, "bias": bias_f.reshape(1, gk)},
                 GLA_HEADS, GLA_DK, GLA_DV)
    gla_o = scan("gla", True, src_gla,
                 {"low": low, "u": u_b, "bias": bias_b.reshape(1, gk), "prev": gla_f,
                  "norm_w": gla_norm_w.reshape(1, GLA_DV)},
                 GLA_HEADS, GLA_DK, GLA_DV)
    hg_f = scan("hgrn", False, dict(src_hg, f=(proj_b, gv + hd)), {"lb_logits": lb_logits_f},
                HGRN_HEADS, HGRN_D, HGRN_D)
    hg_o = scan("hgrn", True, dict(src_hg, f=(proj_b, gv + 2 * hd)),
                {"lb_logits": lb_logits_b, "prev": hg_f,
                 "norm_w": hgrn_norm_w.reshape(1, HGRN_D)},
                HGRN_HEADS, HGRN_D, HGRN_D)
    return outproj(gla_o, hg_o, w_out.astype(BF16), x)


def kernel(x, norm_mix_w, w_in, gla_gate_up_f, gla_gate_bias_f, gla_gate_up_b, gla_gate_bias_b, gla_norm_w, hgrn_lb_logits_f, hgrn_lb_logits_b, hgrn_norm_w, w_out, norm_ffn_w, w_router, expert_w1, expert_w3, expert_w2, norm_final_w):
    b, l, d = x.shape
    outs = []
    for bi in range(b):
        xb = x.reshape(l, d) if b == 1 else x[bi]
        h = mixer_block(xb, norm_mix_w[0], w_in[0], gla_gate_up_f[0], gla_gate_bias_f[0],
                        gla_gate_up_b[0], gla_gate_bias_b[0], gla_norm_w[0],
                        hgrn_lb_logits_f, hgrn_lb_logits_b, hgrn_norm_w[0], w_out[0])
        outs.append(moe_block(h, norm_ffn_w[0], w_router[0], expert_w1[0], expert_w3[0],
                              expert_w2[0], norm_final_w))
    return outs[0].reshape(1, l, d) if b == 1 else jnp.stack(outs, axis=0)
```

```python
import functools

import jax
import jax.numpy as jnp
from jax import lax
from jax.experimental import pallas as pl
from jax.experimental.pallas import tpu as pltpu

F32 = jnp.float32
BF16 = jnp.bfloat16
I32 = jnp.int32

EPS = 1e-6
LANES = 128
F32_SUBLANES = 8
BF16_SUBLANES = 16
VMEM_LIMIT = 56 * 1024 * 1024

GLA_HEADS = 8
GLA_DK = 128
GLA_DV = 256
GLA_GATE_RANK = 16
GLA_GATE_NORMALIZER = 16.0
HGRN_HEADS = 16
HGRN_D = 128
CHUNK = 64
SUB = 8
LOG2E = 1.4426950408889634
N_EXPERTS = 16
CAPACITY_FACTOR = 2

NT_DIMS = (((1,), (1,)), ((), ()))
TN_DIMS = (((0,), (0,)), ((), ()))


def _params(*sem):
    return pltpu.CompilerParams(dimension_semantics=sem, vmem_limit_bytes=VMEM_LIMIT)


def _rmsnorm_kernel(x_ref, w_ref, o_ref):
    x = x_ref[...]
    ms = jnp.mean(x * x, axis=-1, keepdims=True)
    o_ref[...] = (x * lax.rsqrt(ms + EPS) * w_ref[...]).astype(o_ref.dtype)


def rmsnorm(x, w, out_dtype, tm=256):
    n, d = x.shape
    return pl.pallas_call(
        _rmsnorm_kernel,
        grid=(n // tm,),
        in_specs=[pl.BlockSpec((tm, d), lambda i: (i, 0)),
                  pl.BlockSpec((1, d), lambda i: (0, 0))],
        out_specs=pl.BlockSpec((tm, d), lambda i: (i, 0)),
        out_shape=jax.ShapeDtypeStruct((n, d), out_dtype),
        compiler_params=_params("parallel"),
        name="rmsnorm",
    )(x, w.reshape(1, d))


def _mm_nt_kernel(a_ref, w_ref, o_ref):
    o_ref[...] = lax.dot_general(a_ref[...], w_ref[...].astype(BF16), NT_DIMS,
                                 preferred_element_type=F32)


def matmul_nt(a, wt, row0, n, bm, bn, name):
    m, k = a.shape
    bm, bn = min(bm, m), min(bn, n)
    if row0 % bn == 0:
        w_spec = pl.BlockSpec((bn, k), lambda i, j: (row0 // bn + j, 0))
    else:
        assert row0 % F32_SUBLANES == 0 and bn % F32_SUBLANES == 0
        w_spec = pl.BlockSpec(
            (pl.Element(bn), pl.Element(k)),
            lambda i, j: ((row0 // F32_SUBLANES + j * (bn // F32_SUBLANES)) * F32_SUBLANES, 0))
    return pl.pallas_call(
        _mm_nt_kernel,
        grid=(m // bm, n // bn),
        in_specs=[pl.BlockSpec((bm, k), lambda i, j: (i, 0)), w_spec],
        out_specs=pl.BlockSpec((bm, bn), lambda i, j: (i, j)),
        out_shape=jax.ShapeDtypeStruct((m, n), F32),
        compiler_params=_params("parallel", "parallel"),
        name=name,
    )(a, wt)


def _scan_kernel(*refs, kind, rev, tb, dk, dv):
    if kind == "gla":
        q_ref, k_ref, v_ref, low_ref, u_ref, b_ref = refs[:6]
        rest = refs[6:]
    else:
        q_ref, f_ref, v_ref, lbl_ref = refs[:4]
        rest = refs[4:]
    if rev:
        prev_ref, og_ref, nw_ref, o_ref = rest[:4]
        scr = rest[4:]
    else:
        o_ref = rest[0]
        scr = rest[1:]
    st_ref, q_s, k_s, cum_s, o_s = scr

    @pl.when(pl.program_id(1) == 0)
    def _():
        st_ref[...] = jnp.zeros_like(st_ref)

    scale = dk ** -0.5
    if kind == "gla":
        z = jnp.dot(low_ref[...].astype(BF16), u_ref[...],
                    preferred_element_type=F32) + b_ref[...]
        log_sig = jnp.minimum(z, 0.0) - jnp.log1p(jnp.exp(-jnp.abs(z)))
        g = log_sig * (1.0 / GLA_GATE_NORMALIZER)
        q_s[...] = q_ref[...] * scale
        k_s[...] = k_ref[...]
    else:
        logits = lbl_ref[...]
        e = jnp.exp(logits - jnp.max(logits, axis=0, keepdims=True))
        lb = e[0:1] / jnp.sum(e, axis=0, keepdims=True)
        f = f_ref[...]
        k_s[...] = (1.0 - lb) * jax.nn.sigmoid(-f)
        g = jnp.log(lb + (1.0 - lb) * jax.nn.sigmoid(f))
        qq = q_ref[...]
        q_s[...] = qq * jax.nn.sigmoid(qq) * scale

    c = CHUNK
    n_sub = c // SUB
    row_cc = lax.broadcasted_iota(I32, (c, c), 0)
    col_cc = lax.broadcasted_iota(I32, (c, c), 1)
    tri = (col_cc >= row_cc) if rev else (col_cc <= row_cc)
    tri_b = jnp.where(tri, 1.0, 0.0).astype(BF16)
    lane_sc = lax.broadcasted_iota(I32, (SUB, c), 1)
    t_loc = lax.broadcasted_iota(I32, (SUB, 1), 0)

    n_chunk = tb // c
    order = list(range(n_chunk - 1, -1, -1) if rev else range(n_chunk))

    c2s = {}
    for j in order:
        r0 = j * c
        gc = g[r0:r0 + c]
        g_hi = gc.astype(BF16)
        res = gc - g_hi.astype(F32)
        g_mid = res.astype(BF16)
        g_lo = (res - g_mid.astype(F32)).astype(BF16)
        cum = (jnp.dot(tri_b, g_hi, preferred_element_type=F32)
               + jnp.dot(tri_b, g_mid, preferred_element_type=F32)
               + jnp.dot(tri_b, g_lo, preferred_element_type=F32))
        c2 = cum * LOG2E
        cum_s[r0:r0 + c, :] = c2
        c2s[j] = c2

    a_offs = {}
    for j in order:
        r0 = j * c
        c2 = c2s[j]
        qc = q_s[r0:r0 + c]
        kc = k_s[r0:r0 + c]
        a_off = None
        w = c // 2
        while w >= SUB:
            refs = []
            for b in range(0, c, 2 * w):
                row = (b + w) if rev else (b + w - 1)
                refs.append(jnp.broadcast_to(c2[row:row + 1], (2 * w, dk)))
            ref_map = refs[0] if len(refs) == 1 else jnp.concatenate(refs, axis=0)
            fac = jnp.exp2(-jnp.abs(c2 - ref_map))
            zero = jnp.zeros((w, dk), F32)
            q_parts, k_parts = [], []
            for b in range(0, c, 2 * w):
                first, second = slice(b, b + w), slice(b + w, b + 2 * w)
                if rev:
                    q_parts += [qc[first] * fac[first], zero]
                    k_parts += [zero, kc[second] * fac[second]]
                else:
                    q_parts += [zero, qc[second] * fac[second]]
                    k_parts += [kc[first] * fac[first], zero]
            qt = jnp.concatenate(q_parts, axis=0).astype(BF16)
            kt = jnp.concatenate(k_parts, axis=0).astype(BF16)
            a_w = lax.dot_general(qt, kt, NT_DIMS, preferred_element_type=F32)
            if 2 * w < c:
                shift = (2 * w).bit_length() - 1
                same = jnp.right_shift(row_cc, shift) == jnp.right_shift(col_cc, shift)
                a_w = jnp.where(same, a_w, 0.0)
            a_off = a_w if a_off is None else a_off + a_w
            w //= 2
        a_offs[j] = a_off

    st = st_ref[...]
    for j in order:
        r0 = j * c
        c2 = c2s[j]
        qc = q_s[r0:r0 + c]
        kc = k_s[r0:r0 + c]
        vb = v_ref[r0:r0 + c].astype(BF16)
        tot2 = c2[0:1] if rev else c2[c - 1:c]
        qh = (qc * jnp.exp2(c2)).astype(BF16)
        o_s[r0:r0 + c, :] = lax.dot_general(qh, st.astype(BF16), NT_DIMS,
                                            preferred_element_type=F32)
        kdec = (kc * jnp.exp2(tot2 - c2)).astype(BF16)
        st = jnp.exp2(tot2) * st + lax.dot_general(vb, kdec, TN_DIMS,
                                                   preferred_element_type=F32)
    st_ref[...] = st

    for j in order:
        r0 = j * c
        c2 = c2s[j]
        qc = q_s[r0:r0 + c]
        blocks = []
        for j1 in range(n_sub):
            lo = j1 * SUB
            qb = qc[lo:lo + SUB]
            cb = c2[lo:lo + SUB]
            blk = jnp.zeros((SUB, c), F32)
            for sl in range(SUB):
                s = lo + sl
                cs = cum_s[r0 + s:r0 + s + 1, :]
                ks = k_s[r0 + s:r0 + s + 1, :]
                valid_t = (t_loc <= sl) if rev else (t_loc >= sl)
                p = qb * ks * jnp.exp2(jnp.where(valid_t, cb - cs, -jnp.inf))
                col = jnp.sum(p, axis=1, keepdims=True)
                blk = jnp.where(lane_sc == s, col, blk)
            blocks.append(blk)
        a = (jnp.concatenate(blocks, axis=0) + a_offs[j]).astype(BF16)
        vb = v_ref[r0:r0 + c].astype(BF16)
        o_s[r0:r0 + c, :] += jnp.dot(a, vb, preferred_element_type=F32)

    o = o_s[...]
    if rev:
        t = prev_ref[...] + o
        ms = jnp.mean(t * t, axis=-1, keepdims=True)
        y = t * lax.rsqrt(ms + EPS) * nw_ref[...]
        og = og_ref[...]
        o_ref[...] = (y * (og * jax.nn.sigmoid(og))).astype(o_ref.dtype)
    else:
        o_ref[...] = o


def scan(kind, rev, srcs, extras, heads, dk, dv, tb=512):
    n = srcs["q"][0].shape[0]
    tb = min(tb, n)
    nb = n // tb

    def blk(c):
        return (nb - 1 - c) if rev else c

    def col_spec(width, name):
        base = srcs[name][1] // width
        return pl.BlockSpec((tb, width), lambda h, c: (blk(c), base + h))

    if kind == "gla":
        low, u, bias = extras["low"], extras["u"], extras["bias"]
        operands = [srcs["q"][0], srcs["k"][0], srcs["v"][0], low, u, bias]
        in_specs = [col_spec(dk, "q"), col_spec(dk, "k"), col_spec(dv, "v"),
                    pl.BlockSpec((tb, LANES), lambda h, c: (blk(c), 0)),
                    pl.BlockSpec((LANES, dk), lambda h, c: (0, h)),
                    pl.BlockSpec((1, dk), lambda h, c: (0, h))]
    else:
        lbl = extras["lb_logits"]
        operands = [srcs["q"][0], srcs["f"][0], srcs["v"][0], lbl]
        in_specs = [col_spec(dk, "q"), col_spec(dk, "f"), col_spec(dv, "v"),
                    pl.BlockSpec((lbl.shape[0], dk), lambda h, c: (0, h))]
    if rev:
        operands += [extras["prev"], srcs["og"][0], extras["norm_w"]]
        in_specs += [pl.BlockSpec((tb, dv), lambda h, c: (blk(c), h)),
                     col_spec(dv, "og"),
                     pl.BlockSpec((1, dv), lambda h, c: (0, 0))]
    out_dtype = BF16 if rev else F32
    kern = functools.partial(_scan_kernel, kind=kind, rev=rev, tb=tb, dk=dk, dv=dv)
    return pl.pallas_call(
        kern,
        grid=(heads, nb),
        in_specs=in_specs,
        out_specs=pl.BlockSpec((tb, dv), lambda h, c: (blk(c), h)),
        out_shape=jax.ShapeDtypeStruct((n, heads * dv), out_dtype),
        scratch_shapes=[pltpu.VMEM((dv, dk), F32),
                        pltpu.VMEM((tb, dk), F32),
                        pltpu.VMEM((tb, dk), F32),
                        pltpu.VMEM((tb, dk), F32),
                        pltpu.VMEM((tb, dv), F32)],
        compiler_params=_params("parallel", "arbitrary"),
        name=f"scan_{kind}_{'bwd' if rev else 'fwd'}",
    )(*operands)


def _outproj_kernel(a_ref, b_ref, wa_ref, wb_ref, x_ref, o_ref):
    acc = jnp.dot(a_ref[...], wa_ref[...].astype(BF16), preferred_element_type=F32)
    acc += jnp.dot(b_ref[...], wb_ref[...].astype(BF16), preferred_element_type=F32)
    o_ref[...] = x_ref[...] + acc


def outproj(a, b, w, x, bm=512, bn=512):
    m, ka = a.shape
    kb = b.shape[1]
    n = w.shape[1]
    assert ka == kb
    bm, bn = min(bm, m), min(bn, n)
    return pl.pallas_call(
        _outproj_kernel,
        grid=(n // bn, m // bm),
        in_specs=[pl.BlockSpec((bm, ka), lambda j, i: (i, 0)),
                  pl.BlockSpec((bm, kb), lambda j, i: (i, 0)),
                  pl.BlockSpec((ka, bn), lambda j, i: (0, j)),
                  pl.BlockSpec((kb, bn), lambda j, i: (1, j)),
                  pl.BlockSpec((bm, bn), lambda j, i: (i, j))],
        out_specs=pl.BlockSpec((bm, bn), lambda j, i: (i, j)),
        out_shape=jax.ShapeDtypeStruct((m, n), F32),
        compiler_params=_params("parallel", "parallel"),
        name="outproj",
    )(a, b, w, w, x)


def _norm_router_kernel(h_ref, w_ref, wr_ref, hn_ref, aff_ref):
    x = h_ref[...]
    ms = jnp.mean(x * x, axis=-1, keepdims=True)
    hn = x * lax.rsqrt(ms + EPS) * w_ref[...]
    d2 = hn.shape[1] // 2
    hn_ref[...] = pltpu.pack_elementwise([hn[:, :d2], hn[:, d2:]], packed_dtype=BF16)
    logits = lax.dot_general(wr_ref[...], hn, NT_DIMS, preferred_element_type=F32,
                             precision=lax.Precision.HIGHEST)
    m = jnp.max(logits, axis=0, keepdims=True)
    p = jnp.exp(logits - m)
    aff_ref[...] = p / jnp.sum(p, axis=0, keepdims=True)


def norm_router(h, w, w_router_t, tm=256):
    n, d = h.shape
    e = w_router_t.shape[0]
    return pl.pallas_call(
        _norm_router_kernel,
        grid=(n // tm,),
        in_specs=[pl.BlockSpec((tm, d), lambda i: (i, 0)),
                  pl.BlockSpec((1, d), lambda i: (0, 0)),
                  pl.BlockSpec((e, d), lambda i: (0, 0))],
        out_specs=[pl.BlockSpec((tm, d // 2), lambda i: (i, 0)),
                   pl.BlockSpec((e, tm), lambda i: (0, i))],
        out_shape=[jax.ShapeDtypeStruct((n, d // 2), I32),
                   jax.ShapeDtypeStruct((e, n), F32)],
        compiler_params=_params("parallel"),
        name="norm_router",
    )(h, w.reshape(1, d), w_router_t)


def _prefix_excl(m_f, out_ref, tri_b):
    e, n = m_f.shape
    off = jnp.zeros((e, 1), F32)
    for b in range(n // LANES):
        mb = m_f[:, b * LANES:(b + 1) * LANES]
        inc = jnp.dot(mb.astype(BF16), tri_b, preferred_element_type=F32)
        out_ref[:, b * LANES:(b + 1) * LANES] = inc - mb + off
        off = off + inc[:, LANES - 1:LANES]


def _select_kernel(aff_ref, selrank_ref, cnt_ref, eqr_s, *, cap):
    a = aff_ref[...]
    e, n = a.shape
    bits = jnp.zeros((e, 1), I32)
    for b in range(30, -1, -1):
        cand = bits | (1 << b)
        cnt = jnp.sum(jnp.where(a >= pltpu.bitcast(cand, F32), 1.0, 0.0), axis=1, keepdims=True)
        bits = jnp.where(cnt >= cap, cand, bits)
    thr = pltpu.bitcast(bits, F32)
    gt = a > thr
    eq = a == thr
    n_gt = jnp.sum(jnp.where(gt, 1.0, 0.0), axis=1, keepdims=True)
    need = cap - n_gt
    r = lax.broadcasted_iota(I32, (LANES, LANES), 0)
    cidx = lax.broadcasted_iota(I32, (LANES, LANES), 1)
    tri_b = jnp.where(r <= cidx, 1.0, 0.0).astype(BF16)
    _prefix_excl(jnp.where(eq, 1.0, 0.0), eqr_s, tri_b)
    sel = jnp.where(gt, 1.0, jnp.where(eq, jnp.where(eqr_s[...] < need, 1.0, 0.0), 0.0))
    _prefix_excl(sel, eqr_s, tri_b)
    cnt = eqr_s[...].astype(I32)
    cnt_ref[...] = cnt
    selrank_ref[...] = jnp.where(sel > 0.5, cnt, -1)


def select_topk(aff_t, cap):
    e, n = aff_t.shape
    return pl.pallas_call(
        functools.partial(_select_kernel, cap=cap),
        out_shape=[jax.ShapeDtypeStruct((e, n), I32), jax.ShapeDtypeStruct((e, n), I32)],
        scratch_shapes=[pltpu.VMEM((e, n), F32)],
        compiler_params=pltpu.CompilerParams(vmem_limit_bytes=VMEM_LIMIT),
        name="select_topk",
    )(aff_t)


def _compact_kernel(sr_ref, aff_ref, idx_ref, gate_ref, acc_s, *, cap, tc, rt):
    n = sr_ref.shape[-1]
    acc_s[...] = jnp.zeros_like(acc_s)
    rows8 = lax.broadcasted_iota(I32, (8, tc), 0)
    lane8 = lax.broadcasted_iota(I32, (8, tc), 1)
    r_iota = lax.broadcasted_iota(I32, (rt, tc), 0)

    def body(it, carry):
        ci = it // (cap // rt)
        ri = it % (cap // rt)
        c0 = pl.multiple_of(ci * tc, tc)
        q0 = pl.multiple_of(ri * rt, rt)
        sr = sr_ref[0, :, pl.ds(c0, tc)]
        a = aff_ref[0, :, pl.ds(c0, tc)]
        onehot = jnp.where(sr == r_iota + q0, 1.0, 0.0).astype(BF16)
        t = lane8 + c0
        a_hi = a.astype(BF16)
        res = a - a_hi.astype(F32)
        a_mid = res.astype(BF16)
        a_lo = (res - a_mid.astype(F32)).astype(BF16)
        lhs = jnp.where(rows8 == 0, jnp.right_shift(t, 7).astype(F32),
              jnp.where(rows8 == 1, jnp.bitwise_and(t, LANES - 1).astype(F32),
              jnp.where(rows8 == 2, a_hi.astype(F32),
              jnp.where(rows8 == 3, a_mid.astype(F32),
              jnp.where(rows8 == 4, a_lo.astype(F32), 0.0))))).astype(BF16)
        part = lax.dot_general(lhs, onehot, NT_DIMS, preferred_element_type=F32)
        acc_s[:, pl.ds(q0, rt)] += part
        return carry

    lax.fori_loop(0, (n // tc) * (cap // rt), body, 0)
    acc = acc_s[...]
    idx_ref[0] = (acc[0:1] * float(LANES) + acc[1:2]).astype(I32)
    gate_ref[0] = acc[2:3] + acc[3:4] + acc[4:5]


def compact(selrank, aff_t, cap, tc=1024, rt=256):
    e, n = selrank.shape
    tc = min(tc, n)
    rt = min(rt, cap)
    sr3 = selrank.reshape(e, 1, n)
    af3 = aff_t.reshape(e, 1, n)
    idx, gate = pl.pallas_call(
        functools.partial(_compact_kernel, cap=cap, tc=tc, rt=rt),
        grid=(e,),
        in_specs=[pl.BlockSpec((1, 1, n), lambda i: (i, 0, 0)),
                  pl.BlockSpec((1, 1, n), lambda i: (i, 0, 0))],
        out_specs=[pl.BlockSpec((1, 1, cap), lambda i: (i, 0, 0)),
                   pl.BlockSpec((1, 1, cap), lambda i: (i, 0, 0))],
        out_shape=[jax.ShapeDtypeStruct((e, 1, cap), I32),
                   jax.ShapeDtypeStruct((e, 1, cap), F32)],
        scratch_shapes=[pltpu.VMEM((8, cap), F32)],
        compiler_params=_params("parallel"),
        name="compact",
    )(sr3, af3)
    return idx.reshape(e, cap), gate.reshape(e, cap)


def _ffn_kernel(idx_ref, hp_hbm, gate_ref, w1_ref, w3_ref, w2_ref, y_ref,
                xg_s, xb_s, hid_s, sem, *, cap, nf, bf, n_exp):
    e = pl.program_id(0)
    s = pl.program_id(1)
    d2 = xg_s.shape[1]

    def row_copy(row, r):
        return pltpu.make_async_copy(hp_hbm.at[pl.ds(row, 1), :],
                                     xg_s.at[pl.ds(r, 1), :], sem)

    def issue_gather(expert):
        def issue(r, carry):
            row_copy(idx_ref[expert * cap + r], r).start()
            return carry

        lax.fori_loop(0, cap, issue, 0, unroll=8)

    @pl.when(s == 0)
    def _():
        @pl.when(e == 0)
        def _():
            issue_gather(0)

        def drain(r, carry):
            row_copy(0, r).wait()
            return carry

        lax.fori_loop(0, cap, drain, 0, unroll=8)
        xg = xg_s[...]
        xb_s[:, :d2] = pltpu.unpack_elementwise(
            xg, index=0, packed_dtype=BF16, unpacked_dtype=F32).astype(BF16)
        xb_s[:, d2:] = pltpu.unpack_elementwise(
            xg, index=1, packed_dtype=BF16, unpacked_dtype=F32).astype(BF16)

        @pl.when(e + 1 < n_exp)
        def _():
            issue_gather(e + 1)

    @pl.when(s < nf)
    def _():
        x = xb_s[...]
        a = jnp.dot(x, w1_ref[0].astype(BF16), preferred_element_type=F32)
        b = jnp.dot(x, w3_ref[0].astype(BF16), preferred_element_type=F32)
        col = pl.multiple_of(s * bf, bf)
        hid_s[:, pl.ds(col, bf)] = (a * jax.nn.sigmoid(a) * b).astype(BF16)

    @pl.when(s >= nf)
    def _():
        y = jnp.dot(hid_s[...], w2_ref[0].astype(BF16), preferred_element_type=F32)
        y_ref[0] = (y * gate_ref[0]).astype(y_ref.dtype)


def expert_ffn(idx, gate, hn_packed, w1, w3, w2, bf=256, bn=512):
    e, cap = idx.shape
    d2 = hn_packed.shape[1]
    d = 2 * d2
    f = w1.shape[2]
    bf = min(bf, f)
    bn = min(bn, d)
    nf = f // bf
    nn = d // bn
    grid_spec = pltpu.PrefetchScalarGridSpec(
        num_scalar_prefetch=1,
        grid=(e, nf + nn),
        in_specs=[pl.BlockSpec(memory_space=pl.ANY),
                  pl.BlockSpec((1, cap, 1), lambda i, s, idx: (i, 0, 0)),
                  pl.BlockSpec((1, d, bf), lambda i, s, idx: (i, 0, jnp.minimum(s, nf - 1))),
                  pl.BlockSpec((1, d, bf), lambda i, s, idx: (i, 0, jnp.minimum(s, nf - 1))),
                  pl.BlockSpec((1, f, bn), lambda i, s, idx: (i, 0, jnp.maximum(s - nf, 0)))],
        out_specs=pl.BlockSpec((1, cap, bn), lambda i, s, idx: (i, 0, jnp.maximum(s - nf, 0))),
        scratch_shapes=[pltpu.VMEM((cap, d2), I32),
                        pltpu.VMEM((cap, d), BF16),
                        pltpu.VMEM((cap, f), BF16),
                        pltpu.SemaphoreType.DMA(())],
    )
    return pl.pallas_call(
        functools.partial(_ffn_kernel, cap=cap, nf=nf, bf=bf, n_exp=e),
        grid_spec=grid_spec,
        out_shape=jax.ShapeDtypeStruct((e, cap, d), BF16),
        compiler_params=_params("arbitrary", "arbitrary"),
        name="expert_ffn",
    )(idx.reshape(-1), hn_packed, gate.reshape(e, cap, 1), w1, w3, w2)


def _combine_kernel(lo_ref, h_ref, sr_ref, y_hbm, nw_ref, o_ref,
                    acc_s, pack_s, big_s, sem_pack, sem_big,
                    *, n_exp, n_tiles, cap, ws):
    i = pl.program_id(0)
    tt = h_ref.shape[0]
    n_pack = n_exp * ws

    def bounds(tile, e):
        lo = lo_ref[e * (n_tiles + 1) + tile]
        hi = lo_ref[e * (n_tiles + 1) + tile + 1]
        start = jnp.minimum((lo // BF16_SUBLANES) * BF16_SUBLANES, cap - ws)
        return lo, hi, start

    def pack_copy(tile, e, slot):
        _, _, start = bounds(tile, e)
        start = pl.multiple_of(start, BF16_SUBLANES)
        return pltpu.make_async_copy(y_hbm.at[e, pl.ds(start, ws), :],
                                     pack_s.at[slot, pl.ds(e * ws, ws), :],
                                     sem_pack.at[slot])

    def issue(tile, slot):
        for e in range(n_exp):
            pack_copy(tile, e, slot).start()

    @pl.when(i == 0)
    def _():
        issue(0, 0)

    @pl.when(i + 1 < n_tiles)
    def _():
        issue(i + 1, (i + 1) % 2)

    slot = i % 2
    for e in range(n_exp):
        pack_copy(i, e, slot).wait()

    lane = lax.broadcasted_iota(I32, (1, n_pack), 1)
    lane_t = lax.broadcasted_iota(I32, (tt, n_pack), 1)
    target = jnp.full((1, n_pack), -2, I32)
    sr_wide = jnp.full((tt, n_pack), -1, I32)
    for e in range(n_exp):
        lo, hi, start = bounds(i, e)
        covered = hi <= start + ws
        first = jnp.where(covered, start, -(cap + n_pack))
        in_e = (lane >= e * ws) & (lane < (e + 1) * ws)
        target = jnp.where(in_e, first + lane - e * ws, target)
        in_e_t = (lane_t >= e * ws) & (lane_t < (e + 1) * ws)
        sr_wide = jnp.where(in_e_t, sr_ref[:, e:e + 1], sr_wide)
    hit = jnp.where(sr_wide == target, 1.0, 0.0).astype(BF16)
    acc_s[...] = h_ref[...] + jnp.dot(hit, pack_s[slot], preferred_element_type=F32)

    wl = big_s.shape[0]
    j_big = lax.broadcasted_iota(I32, (tt, wl), 1)
    for e in range(n_exp):
        lo, hi, start = bounds(i, e)
        start_l = jnp.minimum(start, cap - wl)
        sr = sr_ref[:, e:e + 1]

        def window(begin, first_slot, e=e, sr=sr):
            begin = pl.multiple_of(begin, BF16_SUBLANES)
            cp = pltpu.make_async_copy(y_hbm.at[e, pl.ds(begin, wl), :], big_s, sem_big)
            cp.start()
            cp.wait()
            slot_id = j_big + begin
            hit_l = jnp.where(sr == slot_id, jnp.where(slot_id >= first_slot, 1.0, 0.0), 0.0)
            acc_s[...] += jnp.dot(hit_l.astype(BF16), big_s[...], preferred_element_type=F32)

        @pl.when(hi > start + ws)
        def _():
            window(start_l, lo)

        @pl.when(hi > start_l + wl)
        def _():
            window(jnp.minimum(start_l + wl, cap - wl), start_l + wl)

    x = acc_s[...]
    ms = jnp.mean(x * x, axis=-1, keepdims=True)
    o_ref[...] = x * lax.rsqrt(ms + EPS) * nw_ref[...]


def combine(h, selrank_t, lo_tab, y, norm_w, tt=256, ws=64):
    n, d = h.shape
    e, cap, _ = y.shape
    tt = min(tt, cap, n)
    ws = min(ws, tt)
    n_tiles = n // tt
    grid_spec = pltpu.PrefetchScalarGridSpec(
        num_scalar_prefetch=1,
        grid=(n_tiles,),
        in_specs=[pl.BlockSpec((tt, d), lambda i, lo: (i, 0)),
                  pl.BlockSpec((tt, e), lambda i, lo: (i, 0)),
                  pl.BlockSpec(memory_space=pl.ANY),
                  pl.BlockSpec((1, d), lambda i, lo: (0, 0))],
        out_specs=pl.BlockSpec((tt, d), lambda i, lo: (i, 0)),
        scratch_shapes=[pltpu.VMEM((tt, d), F32),
                        pltpu.VMEM((2, e * ws, d), BF16),
                        pltpu.VMEM((tt, d), BF16),
                        pltpu.SemaphoreType.DMA((2,)),
                        pltpu.SemaphoreType.DMA(())],
    )
    return pl.pallas_call(
        functools.partial(_combine_kernel, n_exp=e, n_tiles=n_tiles, cap=cap, ws=ws),
        grid_spec=grid_spec,
        out_shape=jax.ShapeDtypeStruct((n, d), F32),
        compiler_params=_params("arbitrary"),
        name="combine",
    )(lo_tab.reshape(-1), h, selrank_t, y, norm_w.reshape(1, d))


def moe_block(h, norm_w, w_router, w1, w3, w2, norm_final_w):
    n, _ = h.shape
    e = w_router.shape[1]
    cap = CAPACITY_FACTOR * n // e
    hn_packed, aff_t = norm_router(h, norm_w, w_router.T)
    selrank, cnt = select_topk(aff_t, cap)
    idx, gate = compact(selrank, aff_t, cap)
    y = expert_ffn(idx, gate, hn_packed, w1, w3, w2)
    tt = min(256, cap, n)
    lo_tab = jnp.concatenate([cnt[:, ::tt], jnp.full((e, 1), cap, I32)], axis=1)
    return combine(h, selrank.T, lo_tab, y, norm_final_w, tt=tt)


def mixer_block(x, norm_w, w_in, up_f, bias_f, up_b, bias_b, gla_norm_w,
                lb_logits_f, lb_logits_b, hgrn_norm_w, w_out):
    gk = GLA_HEADS * GLA_DK
    gv = GLA_HEADS * GLA_DV
    hd = HGRN_HEADS * HGRN_D
    r = GLA_GATE_RANK
    n_a = 2 * gk + gv
    o_b = n_a + 2 * r
    n_b = w_in.shape[1] - o_b
    wt = jnp.swapaxes(w_in, 0, 1)

    xn = rmsnorm(x, norm_w, BF16)
    proj_a = matmul_nt(xn, wt, 0, n_a, 1024, 512, "inproj_a")
    proj_b = matmul_nt(xn, wt, o_b, n_b, 1024, 512, "inproj_b")
    low = matmul_nt(xn, wt, n_a, LANES, 1024, LANES, "inproj_gate")

    src_gla = {"q": (proj_a, 0), "k": (proj_a, gk), "v": (proj_a, 2 * gk), "og": (proj_b, 0)}
    src_hg = {"q": (proj_b, gv), "v": (proj_b, gv + 3 * hd), "og": (proj_b, gv + 4 * hd)}
    u_f = jnp.zeros((LANES, gk), F32).at[0:r].set(up_f).astype(BF16)
    u_b = jnp.zeros((LANES, gk), F32).at[r:2 * r].set(up_b).astype(BF16)
    gla_f = scan("gla", False, src_gla, {"low": low, "u": u_f, "bias": bias_f.reshape(1, gk)},
                 GLA_HEADS, GLA_DK, GLA_DV)
    gla_o = scan("gla", True, src_gla,
                 {"low": low, "u": u_b, "bias": bias_b.reshape(1, gk), "prev": gla_f,
                  "norm_w": gla_norm_w.reshape(1, GLA_DV)},
                 GLA_HEADS, GLA_DK, GLA_DV)
    hg_f = scan("hgrn", False, dict(src_hg, f=(proj_b, gv + hd)), {"lb_logits": lb_logits_f},
                HGRN_HEADS, HGRN_D, HGRN_D)
    hg_o = scan("hgrn", True, dict(src_hg, f=(proj_b, gv + 2 * hd)),
                {"lb_logits": lb_logits_b, "prev": hg_f,
                 "norm_w": hgrn_norm_w.reshape(1, HGRN_D)},
                HGRN_HEADS, HGRN_D, HGRN_D)
    return outproj(gla_o, hg_o, w_out, x)


def kernel(x, norm_mix_w, w_in, gla_gate_up_f, gla_gate_bias_f, gla_gate_up_b, gla_gate_bias_b, gla_norm_w, hgrn_lb_logits_f, hgrn_lb_logits_b, hgrn_norm_w, w_out, norm_ffn_w, w_router, expert_w1, expert_w3, expert_w2, norm_final_w):
    b, l, d = x.shape
    outs = []
    for bi in range(b):
        xb = x.reshape(l, d) if b == 1 else x[bi]
        h = mixer_block(xb, norm_mix_w[0], w_in[0], gla_gate_up_f[0], gla_gate_bias_f[0],
                        gla_gate_up_b[0], gla_gate_bias_b[0], gla_norm_w[0],
                        hgrn_lb_logits_f, hgrn_lb_logits_b, hgrn_norm_w[0], w_out[0])
        outs.append(moe_block(h, norm_ffn_w[0], w_router[0], expert_w1[0], expert_w3[0],
                              expert_w2[0], norm_final_w))
    return outs[0].reshape(1, l, d) if b == 1 else jnp.stack(outs, axis=0)
```

```python
import functools

import jax
import jax.numpy as jnp
from jax import lax
from jax.experimental import pallas as pl
from jax.experimental.pallas import tpu as pltpu

F32 = jnp.float32
BF16 = jnp.bfloat16
I32 = jnp.int32
U32 = jnp.uint32

EPS = 1e-6
LANES = 128
F32_SUBLANES = 8
BF16_SUBLANES = 16
VMEM_LIMIT = 56 * 1024 * 1024

GLA_HEADS = 8
GLA_DK = 128
GLA_DV = 256
GLA_GATE_RANK = 16
GLA_GATE_NORMALIZER = 16.0
HGRN_HEADS = 16
HGRN_D = 128
CHUNK = 64
SUB = 8
LOG2E = 1.4426950408889634
N_EXPERTS = 16
CAPACITY_FACTOR = 2

NT_DIMS = (((1,), (1,)), ((), ()))
TN_DIMS = (((0,), (0,)), ((), ()))


def _params(*sem):
    return pltpu.CompilerParams(dimension_semantics=sem, vmem_limit_bytes=VMEM_LIMIT)


def _rmsnorm_kernel(x_ref, w_ref, o_ref):
    x = x_ref[...]
    ms = jnp.mean(x * x, axis=-1, keepdims=True)
    o_ref[...] = (x * lax.rsqrt(ms + EPS) * w_ref[...]).astype(o_ref.dtype)


def rmsnorm(x, w, out_dtype, tm=256):
    n, d = x.shape
    return pl.pallas_call(
        _rmsnorm_kernel,
        grid=(n // tm,),
        in_specs=[pl.BlockSpec((tm, d), lambda i: (i, 0)),
                  pl.BlockSpec((1, d), lambda i: (0, 0))],
        out_specs=pl.BlockSpec((tm, d), lambda i: (i, 0)),
        out_shape=jax.ShapeDtypeStruct((n, d), out_dtype),
        compiler_params=_params("parallel"),
        name="rmsnorm",
    )(x, w.reshape(1, d))


def _mm_nt_kernel(a_ref, w_ref, o_ref, wb_s):
    @pl.when(pl.program_id(1) == 0)
    def _():
        wb_s[...] = w_ref[...].astype(BF16)

    o_ref[...] = lax.dot_general(a_ref[...], wb_s[...], NT_DIMS, preferred_element_type=F32)


def matmul_nt(a, wt, row0, n, bm, bn, name):
    m, k = a.shape
    bm, bn = min(bm, m), min(bn, n)
    if row0 % bn == 0:
        w_spec = pl.BlockSpec((bn, k), lambda j, i: (row0 // bn + j, 0))
    else:
        assert row0 % F32_SUBLANES == 0 and bn % F32_SUBLANES == 0
        w_spec = pl.BlockSpec(
            (pl.Element(bn), pl.Element(k)),
            lambda j, i: ((row0 // F32_SUBLANES + j * (bn // F32_SUBLANES)) * F32_SUBLANES, 0))
    return pl.pallas_call(
        _mm_nt_kernel,
        grid=(n // bn, m // bm),
        in_specs=[pl.BlockSpec((bm, k), lambda j, i: (i, 0)), w_spec],
        out_specs=pl.BlockSpec((bm, bn), lambda j, i: (i, j)),
        out_shape=jax.ShapeDtypeStruct((m, n), F32),
        scratch_shapes=[pltpu.VMEM((bn, k), BF16)],
        compiler_params=_params("parallel", "arbitrary"),
        name=name,
    )(a, wt)


def _scan_kernel(*refs, kind, rev, tb, dk, dv):
    if kind == "gla":
        q_ref, k_ref, v_ref, low_ref, u_ref, b_ref = refs[:6]
        rest = refs[6:]
    else:
        q_ref, f_ref, v_ref, lbl_ref = refs[:4]
        rest = refs[4:]
    if rev:
        prev_ref, og_ref, nw_ref, o_ref = rest[:4]
        scr = rest[4:]
    else:
        o_ref = rest[0]
        scr = rest[1:]
    st_ref, q_s, k_s, cum_s, o_s = scr

    @pl.when(pl.program_id(1) == 0)
    def _():
        st_ref[...] = jnp.zeros_like(st_ref)

    scale = dk ** -0.5
    if kind == "gla":
        z = jnp.dot(low_ref[...].astype(BF16), u_ref[...],
                    preferred_element_type=F32) + b_ref[...]
        log_sig = jnp.minimum(z, 0.0) - jnp.log1p(jnp.exp(-jnp.abs(z)))
        g = log_sig * (1.0 / GLA_GATE_NORMALIZER)
        q_s[...] = q_ref[...] * scale
        k_s[...] = k_ref[...]
    else:
        logits = lbl_ref[...]
        e = jnp.exp(logits - jnp.max(logits, axis=0, keepdims=True))
        lb = e[0:1] / jnp.sum(e, axis=0, keepdims=True)
        f = f_ref[...]
        k_s[...] = (1.0 - lb) * jax.nn.sigmoid(-f)
        g = jnp.log(lb + (1.0 - lb) * jax.nn.sigmoid(f))
        qq = q_ref[...]
        q_s[...] = qq * jax.nn.sigmoid(qq) * scale

    c = CHUNK
    n_sub = c // SUB
    row_cc = lax.broadcasted_iota(I32, (c, c), 0)
    col_cc = lax.broadcasted_iota(I32, (c, c), 1)
    tri = (col_cc >= row_cc) if rev else (col_cc <= row_cc)
    tri_b = jnp.where(tri, 1.0, 0.0).astype(BF16)
    lane_sc = lax.broadcasted_iota(I32, (SUB, c), 1)
    t_loc = lax.broadcasted_iota(I32, (SUB, 1), 0)

    n_chunk = tb // c
    order = list(range(n_chunk - 1, -1, -1) if rev else range(n_chunk))

    c2s = {}
    for j in order:
        r0 = j * c
        gc = g[r0:r0 + c]
        g_hi = gc.astype(BF16)
        res = gc - g_hi.astype(F32)
        g_mid = res.astype(BF16)
        g_lo = (res - g_mid.astype(F32)).astype(BF16)
        cum = (jnp.dot(tri_b, g_hi, preferred_element_type=F32)
               + jnp.dot(tri_b, g_mid, preferred_element_type=F32)
               + jnp.dot(tri_b, g_lo, preferred_element_type=F32))
        c2 = cum * LOG2E
        cum_s[r0:r0 + c, :] = c2
        c2s[j] = c2

    a_offs = {}
    for j in order:
        r0 = j * c
        c2 = c2s[j]
        qc = q_s[r0:r0 + c]
        kc = k_s[r0:r0 + c]
        a_off = None
        w = c // 2
        while w >= SUB:
            refs = []
            for b in range(0, c, 2 * w):
                row = (b + w) if rev else (b + w - 1)
                refs.append(jnp.broadcast_to(c2[row:row + 1], (2 * w, dk)))
            ref_map = refs[0] if len(refs) == 1 else jnp.concatenate(refs, axis=0)
            fac = jnp.exp2(-jnp.abs(c2 - ref_map))
            zero = jnp.zeros((w, dk), F32)
            q_parts, k_parts = [], []
            for b in range(0, c, 2 * w):
                first, second = slice(b, b + w), slice(b + w, b + 2 * w)
                if rev:
                    q_parts += [qc[first] * fac[first], zero]
                    k_parts += [zero, kc[second] * fac[second]]
                else:
                    q_parts += [zero, qc[second] * fac[second]]
                    k_parts += [kc[first] * fac[first], zero]
            qt = jnp.concatenate(q_parts, axis=0).astype(BF16)
            kt = jnp.concatenate(k_parts, axis=0).astype(BF16)
            a_w = lax.dot_general(qt, kt, NT_DIMS, preferred_element_type=F32)
            if 2 * w < c:
                shift = (2 * w).bit_length() - 1
                same = jnp.right_shift(row_cc, shift) == jnp.right_shift(col_cc, shift)
                a_w = jnp.where(same, a_w, 0.0)
            a_off = a_w if a_off is None else a_off + a_w
            w //= 2
        a_offs[j] = a_off

    st = st_ref[...]
    for j in order:
        r0 = j * c
        c2 = c2s[j]
        qc = q_s[r0:r0 + c]
        kc = k_s[r0:r0 + c]
        vb = v_ref[r0:r0 + c].astype(BF16)
        tot2 = c2[0:1] if rev else c2[c - 1:c]
        qh = (qc * jnp.exp2(c2)).astype(BF16)
        o_s[r0:r0 + c, :] = lax.dot_general(qh, st.astype(BF16), NT_DIMS,
                                            preferred_element_type=F32)
        kdec = (kc * jnp.exp2(tot2 - c2)).astype(BF16)
        st = jnp.exp2(tot2) * st + lax.dot_general(vb, kdec, TN_DIMS,
                                                   preferred_element_type=F32)
    st_ref[...] = st

    for j in order:
        r0 = j * c
        c2 = c2s[j]
        qc = q_s[r0:r0 + c]
        blocks = []
        for j1 in range(n_sub):
            lo = j1 * SUB
            qb = qc[lo:lo + SUB]
            cb = c2[lo:lo + SUB]
            blk = jnp.zeros((SUB, c), F32)
            for sl in range(SUB):
                s = lo + sl
                cs = cum_s[r0 + s:r0 + s + 1, :]
                ks = k_s[r0 + s:r0 + s + 1, :]
                valid_t = (t_loc <= sl) if rev else (t_loc >= sl)
                p = qb * ks * jnp.exp2(jnp.where(valid_t, cb - cs, -jnp.inf))
                col = jnp.sum(p, axis=1, keepdims=True)
                blk = jnp.where(lane_sc == s, col, blk)
            blocks.append(blk)
        a = (jnp.concatenate(blocks, axis=0) + a_offs[j]).astype(BF16)
        vb = v_ref[r0:r0 + c].astype(BF16)
        o_s[r0:r0 + c, :] += jnp.dot(a, vb, preferred_element_type=F32)

    o = o_s[...]
    if rev:
        t = prev_ref[...] + o
        ms = jnp.mean(t * t, axis=-1, keepdims=True)
        y = t * lax.rsqrt(ms + EPS) * nw_ref[...]
        og = og_ref[...]
        o_ref[...] = (y * (og * jax.nn.sigmoid(og))).astype(o_ref.dtype)
    else:
        o_ref[...] = o


def scan(kind, rev, srcs, extras, heads, dk, dv, tb=1024):
    n = srcs["q"][0].shape[0]
    tb = min(tb, n)
    nb = n // tb

    def blk(c):
        return (nb - 1 - c) if rev else c

    def col_spec(width, name):
        base = srcs[name][1] // width
        return pl.BlockSpec((tb, width), lambda h, c: (blk(c), base + h))

    if kind == "gla":
        low, u, bias = extras["low"], extras["u"], extras["bias"]
        operands = [srcs["q"][0], srcs["k"][0], srcs["v"][0], low, u, bias]
        in_specs = [col_spec(dk, "q"), col_spec(dk, "k"), col_spec(dv, "v"),
                    pl.BlockSpec((tb, LANES), lambda h, c: (blk(c), 0)),
                    pl.BlockSpec((LANES, dk), lambda h, c: (0, h)),
                    pl.BlockSpec((1, dk), lambda h, c: (0, h))]
    else:
        lbl = extras["lb_logits"]
        operands = [srcs["q"][0], srcs["f"][0], srcs["v"][0], lbl]
        in_specs = [col_spec(dk, "q"), col_spec(dk, "f"), col_spec(dv, "v"),
                    pl.BlockSpec((lbl.shape[0], dk), lambda h, c: (0, h))]
    if rev:
        operands += [extras["prev"], srcs["og"][0], extras["norm_w"]]
        in_specs += [pl.BlockSpec((tb, dv), lambda h, c: (blk(c), h)),
                     col_spec(dv, "og"),
                     pl.BlockSpec((1, dv), lambda h, c: (0, 0))]
    out_dtype = BF16 if rev else F32
    kern = functools.partial(_scan_kernel, kind=kind, rev=rev, tb=tb, dk=dk, dv=dv)
    return pl.pallas_call(
        kern,
        grid=(heads, nb),
        in_specs=in_specs,
        out_specs=pl.BlockSpec((tb, dv), lambda h, c: (blk(c), h)),
        out_shape=jax.ShapeDtypeStruct((n, heads * dv), out_dtype),
        scratch_shapes=[pltpu.VMEM((dv, dk), F32),
                        pltpu.VMEM((tb, dk), F32),
                        pltpu.VMEM((tb, dk), F32),
                        pltpu.VMEM((tb, dk), F32),
                        pltpu.VMEM((tb, dv), F32)],
        compiler_params=_params("parallel", "arbitrary"),
        name=f"scan_{kind}_{'bwd' if rev else 'fwd'}",
    )(*operands)


def _outproj_kernel(a_ref, b_ref, wa_ref, wb_ref, x_ref, o_ref, wa_s, wb_s):
    @pl.when(pl.program_id(1) == 0)
    def _():
        wa_s[...] = wa_ref[...].astype(BF16)
        wb_s[...] = wb_ref[...].astype(BF16)

    acc = jnp.dot(a_ref[...], wa_s[...], preferred_element_type=F32)
    acc += jnp.dot(b_ref[...], wb_s[...], preferred_element_type=F32)
    o_ref[...] = x_ref[...] + acc


def outproj(a, b, w, x, bm=1024, bn=512):
    m, ka = a.shape
    kb = b.shape[1]
    n = w.shape[1]
    assert ka == kb
    bm, bn = min(bm, m), min(bn, n)
    return pl.pallas_call(
        _outproj_kernel,
        grid=(n // bn, m // bm),
        in_specs=[pl.BlockSpec((bm, ka), lambda j, i: (i, 0)),
                  pl.BlockSpec((bm, kb), lambda j, i: (i, 0)),
                  pl.BlockSpec((ka, bn), lambda j, i: (0, j)),
                  pl.BlockSpec((kb, bn), lambda j, i: (1, j)),
                  pl.BlockSpec((bm, bn), lambda j, i: (i, j))],
        out_specs=pl.BlockSpec((bm, bn), lambda j, i: (i, j)),
        out_shape=jax.ShapeDtypeStruct((m, n), F32),
        scratch_shapes=[pltpu.VMEM((ka, bn), BF16), pltpu.VMEM((kb, bn), BF16)],
        compiler_params=_params("parallel", "arbitrary"),
        name="outproj",
    )(a, b, w, w, x)


def _norm_router_kernel(h_ref, w_ref, wr_ref, hn_ref, aff_ref):
    x = h_ref[...]
    ms = jnp.mean(x * x, axis=-1, keepdims=True)
    hn = x * lax.rsqrt(ms + EPS) * w_ref[...]
    d2 = hn.shape[1] // 2
    hn_ref[...] = pltpu.bitcast(
        pltpu.pack_elementwise([hn[:, :d2], hn[:, d2:]], packed_dtype=BF16), U32)
    hn_hi = hn.astype(BF16)
    hn_lo = (hn - hn_hi.astype(F32)).astype(BF16)
    wr = wr_ref[...]
    wr_hi = wr.astype(BF16)
    wr_lo = (wr - wr_hi.astype(F32)).astype(BF16)
    logits = (lax.dot_general(wr_hi, hn_hi, NT_DIMS, preferred_element_type=F32)
              + lax.dot_general(wr_hi, hn_lo, NT_DIMS, preferred_element_type=F32)
              + lax.dot_general(wr_lo, hn_hi, NT_DIMS, preferred_element_type=F32))
    m = jnp.max(logits, axis=0, keepdims=True)
    p = jnp.exp(logits - m)
    aff_ref[...] = p / jnp.sum(p, axis=0, keepdims=True)


def norm_router(h, w, w_router_t, tm=256):
    n, d = h.shape
    e = w_router_t.shape[0]
    return pl.pallas_call(
        _norm_router_kernel,
        grid=(n // tm,),
        in_specs=[pl.BlockSpec((tm, d), lambda i: (i, 0)),
                  pl.BlockSpec((1, d), lambda i: (0, 0)),
                  pl.BlockSpec((e, d), lambda i: (0, 0))],
        out_specs=[pl.BlockSpec((tm, d // 2), lambda i: (i, 0)),
                   pl.BlockSpec((e, tm), lambda i: (0, i))],
        out_shape=[jax.ShapeDtypeStruct((n, d // 2), U32),
                   jax.ShapeDtypeStruct((e, n), F32)],
        compiler_params=_params("parallel"),
        name="norm_router",
    )(h, w.reshape(1, d), w_router_t)


def _prefix_excl(m_f, out_ref, tri_b):
    e, n = m_f.shape
    off = jnp.zeros((e, 1), F32)
    for b in range(n // LANES):
        mb = m_f[:, b * LANES:(b + 1) * LANES]
        inc = jnp.dot(mb.astype(BF16), tri_b, preferred_element_type=F32)
        out_ref[:, b * LANES:(b + 1) * LANES] = inc - mb + off
        off = off + inc[:, LANES - 1:LANES]


def _select_kernel(aff_ref, selrank_ref, cnt_ref, eqr_s, *, cap):
    a = aff_ref[...]
    e, n = a.shape
    bits = jnp.zeros((e, 1), I32)
    for b in range(30, -1, -1):
        cand = bits | (1 << b)
        cnt = jnp.sum(jnp.where(a >= pltpu.bitcast(cand, F32), 1.0, 0.0), axis=1, keepdims=True)
        bits = jnp.where(cnt >= cap, cand, bits)
    thr = pltpu.bitcast(bits, F32)
    gt = a > thr
    eq = a == thr
    n_gt = jnp.sum(jnp.where(gt, 1.0, 0.0), axis=1, keepdims=True)
    need = cap - n_gt
    r = lax.broadcasted_iota(I32, (LANES, LANES), 0)
    cidx = lax.broadcasted_iota(I32, (LANES, LANES), 1)
    tri_b = jnp.where(r <= cidx, 1.0, 0.0).astype(BF16)
    _prefix_excl(jnp.where(eq, 1.0, 0.0), eqr_s, tri_b)
    sel = jnp.where(gt, 1.0, jnp.where(eq, jnp.where(eqr_s[...] < need, 1.0, 0.0), 0.0))
    _prefix_excl(sel, eqr_s, tri_b)
    cnt = eqr_s[...].astype(I32)
    cnt_ref[...] = cnt
    selrank_ref[...] = jnp.where(sel > 0.5, cnt, -1)


def select_topk(aff_t, cap):
    e, n = aff_t.shape
    return pl.pallas_call(
        functools.partial(_select_kernel, cap=cap),
        out_shape=[jax.ShapeDtypeStruct((e, n), I32), jax.ShapeDtypeStruct((e, n), I32)],
        scratch_shapes=[pltpu.VMEM((e, n), F32)],
        compiler_params=pltpu.CompilerParams(vmem_limit_bytes=VMEM_LIMIT),
        name="select_topk",
    )(aff_t)


def _compact_kernel(lo_ref, sr_ref, aff_ref, idx_ref, gate_ref, acc_s, *, cap, tc, rt):
    n = sr_ref.shape[-1]
    n_c = n // tc
    e = pl.program_id(0)
    acc_s[...] = jnp.zeros_like(acc_s)
    rows8 = lax.broadcasted_iota(I32, (8, tc), 0)
    lane8 = lax.broadcasted_iota(I32, (8, tc), 1)
    r_iota = lax.broadcasted_iota(I32, (rt, tc), 0)

    def body(it, carry):
        ci = it // (cap // rt)
        ri = it % (cap // rt)
        c0 = pl.multiple_of(ci * tc, tc)
        q0 = pl.multiple_of(ri * rt, rt)
        lo = lo_ref[e * (n_c + 1) + ci]
        hi = lo_ref[e * (n_c + 1) + ci + 1]

        @pl.when((hi > q0) & (lo < q0 + rt))
        def _():
            sr = sr_ref[0, :, pl.ds(c0, tc)]
            a = aff_ref[0, :, pl.ds(c0, tc)]
            onehot = jnp.where(sr == r_iota + q0, 1.0, 0.0).astype(BF16)
            t = lane8 + c0
            a_hi = a.astype(BF16)
            res = a - a_hi.astype(F32)
            a_mid = res.astype(BF16)
            a_lo = (res - a_mid.astype(F32)).astype(BF16)
            lhs = jnp.where(rows8 == 0, jnp.right_shift(t, 7).astype(F32),
                  jnp.where(rows8 == 1, jnp.bitwise_and(t, LANES - 1).astype(F32),
                  jnp.where(rows8 == 2, a_hi.astype(F32),
                  jnp.where(rows8 == 3, a_mid.astype(F32),
                  jnp.where(rows8 == 4, a_lo.astype(F32), 0.0))))).astype(BF16)
            part = lax.dot_general(lhs, onehot, NT_DIMS, preferred_element_type=F32)
            acc_s[:, pl.ds(q0, rt)] += part

        return carry

    lax.fori_loop(0, n_c * (cap // rt), body, 0)
    acc = acc_s[...]
    idx_ref[0] = (acc[0:1] * float(LANES) + acc[1:2]).astype(I32)
    gate_ref[0] = acc[2:3] + acc[3:4] + acc[4:5]


def compact(selrank, cnt, aff_t, cap, tc=1024, rt=256):
    e, n = selrank.shape
    tc = min(tc, n)
    rt = min(rt, cap)
    sr3 = selrank.reshape(e, 1, n)
    af3 = aff_t.reshape(e, 1, n)
    lo_tab = jnp.concatenate([cnt[:, ::tc], jnp.full((e, 1), cap, I32)], axis=1)
    grid_spec = pltpu.PrefetchScalarGridSpec(
        num_scalar_prefetch=1,
        grid=(e,),
        in_specs=[pl.BlockSpec((1, 1, n), lambda i, lo: (i, 0, 0)),
                  pl.BlockSpec((1, 1, n), lambda i, lo: (i, 0, 0))],
        out_specs=[pl.BlockSpec((1, 1, cap), lambda i, lo: (i, 0, 0)),
                   pl.BlockSpec((1, 1, cap), lambda i, lo: (i, 0, 0))],
        scratch_shapes=[pltpu.VMEM((8, cap), F32)],
    )
    idx, gate = pl.pallas_call(
        functools.partial(_compact_kernel, cap=cap, tc=tc, rt=rt),
        grid_spec=grid_spec,
        out_shape=[jax.ShapeDtypeStruct((e, 1, cap), I32),
                   jax.ShapeDtypeStruct((e, 1, cap), F32)],
        compiler_params=_params("arbitrary"),
        name="compact",
    )(lo_tab.reshape(-1), sr3, af3)
    return idx.reshape(e, cap), gate.reshape(e, cap)


def _ffn_kernel(idx_ref, hp_hbm, gate_ref, w1_ref, w3_ref, w2_ref, y_ref,
                xg_s, xb_s, hid_s, sem, *, cap, nf, bf, n_exp):
    e = pl.program_id(0)
    s = pl.program_id(1)
    d2 = xg_s.shape[1]

    def row_copy(row, r):
        return pltpu.make_async_copy(hp_hbm.at[pl.ds(row, 1), :],
                                     xg_s.at[pl.ds(r, 1), :], sem)

    @pl.when(s == 0)
    def _():
        @pl.when(e == 0)
        def _():
            def issue(r, carry):
                row_copy(idx_ref[r], r).start()
                return carry

            lax.fori_loop(0, cap, issue, 0, unroll=8)

        pltpu.make_async_copy(hp_hbm.at[pl.ds(0, cap), :], xg_s, sem).wait()
        xg = xg_s[...]
        xb_s[:, :d2] = pltpu.unpack_elementwise(
            xg, index=0, packed_dtype=BF16, unpacked_dtype=F32).astype(BF16)
        xb_s[:, d2:] = pltpu.unpack_elementwise(
            xg, index=1, packed_dtype=BF16, unpacked_dtype=F32).astype(BF16)

    @pl.when(s < nf)
    def _():
        nxt = jnp.minimum(e + 1, n_exp - 1)
        share = cap // nf
        for r in range(share):
            slot = s * share + r
            row_copy(idx_ref[nxt * cap + slot], slot).start()

        x = xb_s[...]
        a = jnp.dot(x, w1_ref[0].astype(BF16), preferred_element_type=F32)
        b = jnp.dot(x, w3_ref[0].astype(BF16), preferred_element_type=F32)
        col = pl.multiple_of(s * bf, bf)
        hid_s[:, pl.ds(col, bf)] = (a * jax.nn.sigmoid(a) * b).astype(BF16)

    @pl.when(s >= nf)
    def _():
        y = jnp.dot(hid_s[...], w2_ref[0].astype(BF16), preferred_element_type=F32)
        y_ref[0] = (y * gate_ref[0]).astype(y_ref.dtype)

    @pl.when((e == n_exp - 1) & (s == pl.num_programs(1) - 1))
    def _():
        pltpu.make_async_copy(hp_hbm.at[pl.ds(0, cap), :], xg_s, sem).wait()


def expert_ffn(idx, gate, hn_packed, w1, w3, w2, bf=256, bn=512):
    e, cap = idx.shape
    d2 = hn_packed.shape[1]
    d = 2 * d2
    f = w1.shape[2]
    bf = min(bf, f)
    bn = min(bn, d)
    nf = f // bf
    nn = d // bn
    grid_spec = pltpu.PrefetchScalarGridSpec(
        num_scalar_prefetch=1,
        grid=(e, nf + nn),
        in_specs=[pl.BlockSpec(memory_space=pl.ANY),
                  pl.BlockSpec((1, cap, 1), lambda i, s, idx: (i, 0, 0)),
                  pl.BlockSpec((1, d, bf), lambda i, s, idx: (i, 0, jnp.minimum(s, nf - 1))),
                  pl.BlockSpec((1, d, bf), lambda i, s, idx: (i, 0, jnp.minimum(s, nf - 1))),
                  pl.BlockSpec((1, f, bn), lambda i, s, idx: (i, 0, jnp.maximum(s - nf, 0)))],
        out_specs=pl.BlockSpec((1, cap, bn), lambda i, s, idx: (i, 0, jnp.maximum(s - nf, 0))),
        scratch_shapes=[pltpu.VMEM((cap, d2), U32),
                        pltpu.VMEM((cap, d), BF16),
                        pltpu.VMEM((cap, f), BF16),
                        pltpu.SemaphoreType.DMA(())],
    )
    return pl.pallas_call(
        functools.partial(_ffn_kernel, cap=cap, nf=nf, bf=bf, n_exp=e),
        grid_spec=grid_spec,
        out_shape=jax.ShapeDtypeStruct((e, cap, d), BF16),
        compiler_params=_params("arbitrary", "arbitrary"),
        name="expert_ffn",
    )(idx.reshape(-1), hn_packed, gate.reshape(e, cap, 1), w1, w3, w2)


def _combine_kernel(lo_ref, h_ref, sr_ref, y_hbm, nw_ref, o_ref,
                    acc_s, pack_s, big_s, sem_pack, sem_big,
                    *, n_exp, n_tiles, cap, ws):
    i = pl.program_id(0)
    tt = h_ref.shape[0]
    n_pack = n_exp * ws

    def bounds(tile, e):
        lo = lo_ref[e * (n_tiles + 1) + tile]
        hi = lo_ref[e * (n_tiles + 1) + tile + 1]
        start = jnp.minimum((lo // BF16_SUBLANES) * BF16_SUBLANES, cap - ws)
        return lo, hi, start

    def pack_copy(tile, e, slot):
        _, _, start = bounds(tile, e)
        start = pl.multiple_of(start, BF16_SUBLANES)
        return pltpu.make_async_copy(y_hbm.at[e, pl.ds(start, ws), :],
                                     pack_s.at[slot, pl.ds(e * ws, ws), :],
                                     sem_pack.at[slot])

    def issue(tile, slot):
        for e in range(n_exp):
            pack_copy(tile, e, slot).start()

    @pl.when(i == 0)
    def _():
        issue(0, 0)

    @pl.when(i + 1 < n_tiles)
    def _():
        issue(i + 1, (i + 1) % 2)

    slot = i % 2
    for e in range(n_exp):
        pack_copy(i, e, slot).wait()

    lane = lax.broadcasted_iota(I32, (1, n_pack), 1)
    lane_t = lax.broadcasted_iota(I32, (tt, n_pack), 1)
    target = jnp.full((1, n_pack), -2, I32)
    sr_wide = jnp.full((tt, n_pack), -1, I32)
    for e in range(n_exp):
        lo, hi, start = bounds(i, e)
        covered = hi <= start + ws
        first = jnp.where(covered, start, -(cap + n_pack))
        in_e = (lane >= e * ws) & (lane < (e + 1) * ws)
        target = jnp.where(in_e, first + lane - e * ws, target)
        in_e_t = (lane_t >= e * ws) & (lane_t < (e + 1) * ws)
        sr_wide = jnp.where(in_e_t, sr_ref[:, e:e + 1], sr_wide)
    hit = jnp.where(sr_wide == target, 1.0, 0.0).astype(BF16)
    acc_s[...] = h_ref[...] + jnp.dot(hit, pack_s[slot], preferred_element_type=F32)

    wl = big_s.shape[0]
    j_big = lax.broadcasted_iota(I32, (tt, wl), 1)
    for e in range(n_exp):
        lo, hi, start = bounds(i, e)
        start_l = jnp.minimum(start, cap - wl)
        sr = sr_ref[:, e:e + 1]

        def window(begin, first_slot, e=e, sr=sr):
            begin = pl.multiple_of(begin, BF16_SUBLANES)
            cp = pltpu.make_async_copy(y_hbm.at[e, pl.ds(begin, wl), :], big_s, sem_big)
            cp.start()
            cp.wait()
            slot_id = j_big + begin
            hit_l = jnp.where(sr == slot_id, jnp.where(slot_id >= first_slot, 1.0, 0.0), 0.0)
            acc_s[...] += jnp.dot(hit_l.astype(BF16), big_s[...], preferred_element_type=F32)

        @pl.when(hi > start + ws)
        def _():
            window(start_l, lo)

        @pl.when(hi > start_l + wl)
        def _():
            window(jnp.minimum(start_l + wl, cap - wl), start_l + wl)

    x = acc_s[...]
    ms = jnp.mean(x * x, axis=-1, keepdims=True)
    o_ref[...] = x * lax.rsqrt(ms + EPS) * nw_ref[...]


def combine(h, selrank_t, lo_tab, y, norm_w, tt=256, ws=64):
    n, d = h.shape
    e, cap, _ = y.shape
    tt = min(tt, cap, n)
    ws = min(ws, tt)
    n_tiles = n // tt
    grid_spec = pltpu.PrefetchScalarGridSpec(
        num_scalar_prefetch=1,
        grid=(n_tiles,),
        in_specs=[pl.BlockSpec((tt, d), lambda i, lo: (i, 0)),
                  pl.BlockSpec((tt, e), lambda i, lo: (i, 0)),
                  pl.BlockSpec(memory_space=pl.ANY),
                  pl.BlockSpec((1, d), lambda i, lo: (0, 0))],
        out_specs=pl.BlockSpec((tt, d), lambda i, lo: (i, 0)),
        scratch_shapes=[pltpu.VMEM((tt, d), F32),
                        pltpu.VMEM((2, e * ws, d), BF16),
                        pltpu.VMEM((tt, d), BF16),
                        pltpu.SemaphoreType.DMA((2,)),
                        pltpu.SemaphoreType.DMA(())],
    )
    return pl.pallas_call(
        functools.partial(_combine_kernel, n_exp=e, n_tiles=n_tiles, cap=cap, ws=ws),
        grid_spec=grid_spec,
        out_shape=jax.ShapeDtypeStruct((n, d), F32),
        compiler_params=_params("arbitrary"),
        name="combine",
    )(lo_tab.reshape(-1), h, selrank_t, y, norm_w.reshape(1, d))


def moe_block(h, norm_w, w_router, w1, w3, w2, norm_final_w):
    n, _ = h.shape
    e = w_router.shape[1]
    cap = CAPACITY_FACTOR * n // e
    hn_packed, aff_t = norm_router(h, norm_w, w_router.T)
    selrank, cnt = select_topk(aff_t, cap)
    idx, gate = compact(selrank, cnt, aff_t, cap)
    y = expert_ffn(idx, gate, hn_packed, w1, w3, w2)
    tt = min(256, cap, n)
    lo_tab = jnp.concatenate([cnt[:, ::tt], jnp.full((e, 1), cap, I32)], axis=1)
    return combine(h, selrank.T, lo_tab, y, norm_final_w, tt=tt)


def mixer_block(x, norm_w, w_in, up_f, bias_f, up_b, bias_b, gla_norm_w,
                lb_logits_f, lb_logits_b, hgrn_norm_w, w_out):
    gk = GLA_HEADS * GLA_DK
    gv = GLA_HEADS * GLA_DV
    hd = HGRN_HEADS * HGRN_D
    r = GLA_GATE_RANK
    n_a = 2 * gk + gv
    o_b = n_a + 2 * r
    n_b = w_in.shape[1] - o_b
    wt = jnp.swapaxes(w_in, 0, 1)

    xn = rmsnorm(x, norm_w, BF16)
    proj_a = matmul_nt(xn, wt, 0, n_a, 1024, 512, "inproj_a")
    proj_b = matmul_nt(xn, wt, o_b, n_b, 1024, 512, "inproj_b")
    low = matmul_nt(xn, wt, n_a, LANES, 1024, LANES, "inproj_gate")

    src_gla = {"q": (proj_a, 0), "k": (proj_a, gk), "v": (proj_a, 2 * gk), "og": (proj_b, 0)}
    src_hg = {"q": (proj_b, gv), "v": (proj_b, gv + 3 * hd), "og": (proj_b, gv + 4 * hd)}
    u_f = jnp.zeros((LANES, gk), F32).at[0:r].set(up_f).astype(BF16)
    u_b = jnp.zeros((LANES, gk), F32).at[r:2 * r].set(up_b).astype(BF16)
    gla_f = scan("gla", False, src_gla, {"low": low, "u": u_f, "bias": bias_f.reshape(1, gk)},
                 GLA_HEADS, GLA_DK, GLA_DV)
    gla_o = scan("gla", True, src_gla,
                 {"low": low, "u": u_b, "bias": bias_b.reshape(1, gk), "prev": gla_f,
                  "norm_w": gla_norm_w.reshape(1, GLA_DV)},
                 GLA_HEADS, GLA_DK, GLA_DV)
    hg_f = scan("hgrn", False, dict(src_hg, f=(proj_b, gv + hd)), {"lb_logits": lb_logits_f},
                HGRN_HEADS, HGRN_D, HGRN_D)
    hg_o = scan("hgrn", True, dict(src_hg, f=(proj_b, gv + 2 * hd)),
                {"lb_logits": lb_logits_b, "prev": hg_f,
                 "norm_w": hgrn_norm_w.reshape(1, HGRN_D)},
                HGRN_HEADS, HGRN_D, HGRN_D)
    return outproj(gla_o, hg_o, w_out, x)


def kernel(x, norm_mix_w, w_in, gla_gate_up_f, gla_gate_bias_f, gla_gate_up_b, gla_gate_bias_b, gla_norm_w, hgrn_lb_logits_f, hgrn_lb_logits_b, hgrn_norm_w, w_out, norm_ffn_w, w_router, expert_w1, expert_w3, expert_w2, norm_final_w):
    b, l, d = x.shape
    outs = []
    for bi in range(b):
        xb = x.reshape(l, d) if b == 1 else x[bi]
        h = mixer_block(xb, norm_mix_w[0], w_in[0], gla_gate_up_f[0], gla_gate_bias_f[0],
                        gla_gate_up_b[0], gla_gate_bias_b[0], gla_norm_w[0],
                        hgrn_lb_logits_f, hgrn_lb_logits_b, hgrn_norm_w[0], w_out[0])
        outs.append(moe_block(h, norm_ffn_w[0], w_router[0], expert_w1[0], expert_w3[0],
                              expert_w2[0], norm_final_w))
    return outs[0].reshape(1, l, d) if b == 1 else jnp.stack(outs, axis=0)
```

```python
import functools

import jax
import jax.numpy as jnp
from jax import lax
from jax.experimental import pallas as pl
from jax.experimental.pallas import tpu as pltpu

F32 = jnp.float32
BF16 = jnp.bfloat16
I32 = jnp.int32
U32 = jnp.uint32

EPS = 1e-6
LANES = 128
F32_SUBLANES = 8
BF16_SUBLANES = 16
VMEM_LIMIT = 56 * 1024 * 1024

GLA_HEADS = 8
GLA_DK = 128
GLA_DV = 256
GLA_GATE_RANK = 16
GLA_GATE_NORMALIZER = 16.0
HGRN_HEADS = 16
HGRN_D = 128
CHUNK = 64
SUB = 8
LOG2E = 1.4426950408889634
FAST_STEP_BITS = 12.0
N_EXPERTS = 16
CAPACITY_FACTOR = 2

NT_DIMS = (((1,), (1,)), ((), ()))
TN_DIMS = (((0,), (0,)), ((), ()))


def _params(*sem):
    return pltpu.CompilerParams(dimension_semantics=sem, vmem_limit_bytes=VMEM_LIMIT)


def _rmsnorm_kernel(x_ref, w_ref, o_ref):
    x = x_ref[...]
    ms = jnp.mean(x * x, axis=-1, keepdims=True)
    o_ref[...] = (x * lax.rsqrt(ms + EPS) * w_ref[...]).astype(o_ref.dtype)


def rmsnorm(x, w, out_dtype, tm=256):
    n, d = x.shape
    return pl.pallas_call(
        _rmsnorm_kernel,
        grid=(n // tm,),
        in_specs=[pl.BlockSpec((tm, d), lambda i: (i, 0)),
                  pl.BlockSpec((1, d), lambda i: (0, 0))],
        out_specs=pl.BlockSpec((tm, d), lambda i: (i, 0)),
        out_shape=jax.ShapeDtypeStruct((n, d), out_dtype),
        compiler_params=_params("parallel"),
        name="rmsnorm",
    )(x, w.reshape(1, d))


def _mm_nt_kernel(a_ref, w_ref, o_ref, wb_s):
    @pl.when(pl.program_id(1) == 0)
    def _():
        wb_s[...] = w_ref[...].astype(BF16)

    o_ref[...] = lax.dot_general(a_ref[...], wb_s[...], NT_DIMS, preferred_element_type=F32)


def matmul_nt(a, wt, row0, n, bm, bn, name):
    m, k = a.shape
    bm, bn = min(bm, m), min(bn, n)
    if row0 % bn == 0:
        w_spec = pl.BlockSpec((bn, k), lambda j, i: (row0 // bn + j, 0))
    else:
        assert row0 % F32_SUBLANES == 0 and bn % F32_SUBLANES == 0
        w_spec = pl.BlockSpec(
            (pl.Element(bn), pl.Element(k)),
            lambda j, i: ((row0 // F32_SUBLANES + j * (bn // F32_SUBLANES)) * F32_SUBLANES, 0))
    return pl.pallas_call(
        _mm_nt_kernel,
        grid=(n // bn, m // bm),
        in_specs=[pl.BlockSpec((bm, k), lambda j, i: (i, 0)), w_spec],
        out_specs=pl.BlockSpec((bm, bn), lambda j, i: (i, j)),
        out_shape=jax.ShapeDtypeStruct((m, n), F32),
        scratch_shapes=[pltpu.VMEM((bn, k), BF16)],
        compiler_params=_params("parallel", "arbitrary"),
        name=name,
    )(a, wt)


def _scan_kernel(*refs, kind, rev, tb, dk, dv):
    if kind == "gla":
        q_ref, k_ref, v_ref, low_ref, u_ref, b_ref = refs[:6]
        rest = refs[6:]
    else:
        q_ref, f_ref, v_ref, lbl_ref = refs[:4]
        rest = refs[4:]
    if rev:
        prev_ref, og_ref, nw_ref, o_ref = rest[:4]
        scr = rest[4:]
    else:
        o_ref = rest[0]
        scr = rest[1:]
    st_ref, q_s, k_s, cum_s, a_s, m_s, o_s = scr

    @pl.when(pl.program_id(1) == 0)
    def _():
        st_ref[...] = jnp.zeros_like(st_ref)

    scale = dk ** -0.5
    if kind == "gla":
        z = jnp.dot(low_ref[...].astype(BF16), u_ref[...],
                    preferred_element_type=F32) + b_ref[...]
        log_sig = jnp.minimum(z, 0.0) - jnp.log1p(jnp.exp(-jnp.abs(z)))
        g = log_sig * (1.0 / GLA_GATE_NORMALIZER)
        q_s[...] = q_ref[...] * scale
        k_s[...] = k_ref[...]
    else:
        logits = lbl_ref[...]
        e = jnp.exp(logits - jnp.max(logits, axis=0, keepdims=True))
        lb = e[0:1] / jnp.sum(e, axis=0, keepdims=True)
        f = f_ref[...]
        t = jnp.exp(-jnp.abs(f))
        s_big = 1.0 / (1.0 + t)
        s_small = t * s_big
        k_s[...] = (1.0 - lb) * jnp.where(f >= 0.0, s_small, s_big)
        g = jnp.log(lb + (1.0 - lb) * jnp.where(f >= 0.0, s_big, s_small))
        qq = q_ref[...]
        q_s[...] = qq * jax.nn.sigmoid(qq) * scale
    worst_bits = jnp.max(-g) * LOG2E

    c = CHUNK
    n_sub = c // SUB
    row_cc = lax.broadcasted_iota(I32, (c, c), 0)
    col_cc = lax.broadcasted_iota(I32, (c, c), 1)
    tri = (col_cc >= row_cc) if rev else (col_cc <= row_cc)
    tri_b = jnp.where(tri, 1.0, 0.0).astype(BF16)
    lane_sc = lax.broadcasted_iota(I32, (SUB, c), 1)
    t_loc = lax.broadcasted_iota(I32, (SUB, 1), 0)

    n_chunk = tb // c
    order = list(range(n_chunk - 1, -1, -1) if rev else range(n_chunk))

    c2s = {}
    for j in order:
        r0 = j * c
        gc = g[r0:r0 + c]
        g_hi = gc.astype(BF16)
        res = gc - g_hi.astype(F32)
        g_mid = res.astype(BF16)
        g_lo = (res - g_mid.astype(F32)).astype(BF16)
        cum = (jnp.dot(tri_b, g_hi, preferred_element_type=F32)
               + jnp.dot(tri_b, g_mid, preferred_element_type=F32)
               + jnp.dot(tri_b, g_lo, preferred_element_type=F32))
        c2 = cum * LOG2E
        cum_s[r0:r0 + c, :] = c2
        c2s[j] = c2

    a_offs = {}
    for j in order:
        r0 = j * c
        c2 = c2s[j]
        qc = q_s[r0:r0 + c]
        kc = k_s[r0:r0 + c]
        a_off = None
        w = c // 2
        while w >= SUB:
            refs = []
            for b in range(0, c, 2 * w):
                row = (b + w) if rev else (b + w - 1)
                refs.append(jnp.broadcast_to(c2[row:row + 1], (2 * w, dk)))
            ref_map = refs[0] if len(refs) == 1 else jnp.concatenate(refs, axis=0)
            fac = jnp.exp2(-jnp.abs(c2 - ref_map))
            zero = jnp.zeros((w, dk), F32)
            q_parts, k_parts = [], []
            for b in range(0, c, 2 * w):
                first, second = slice(b, b + w), slice(b + w, b + 2 * w)
                if rev:
                    q_parts += [qc[first] * fac[first], zero]
                    k_parts += [zero, kc[second] * fac[second]]
                else:
                    q_parts += [zero, qc[second] * fac[second]]
                    k_parts += [kc[first] * fac[first], zero]
            qt = jnp.concatenate(q_parts, axis=0).astype(BF16)
            kt = jnp.concatenate(k_parts, axis=0).astype(BF16)
            a_w = lax.dot_general(qt, kt, NT_DIMS, preferred_element_type=F32)
            if 2 * w < c:
                shift = (2 * w).bit_length() - 1
                same = jnp.right_shift(row_cc, shift) == jnp.right_shift(col_cc, shift)
                a_w = jnp.where(same, a_w, 0.0)
            a_off = a_w if a_off is None else a_off + a_w
            w //= 2
        a_offs[j] = a_off

    for j in order:
        r0 = j * c
        c2 = c2s[j]
        tot2 = c2[0:1] if rev else c2[c - 1:c]
        kdec = (k_s[r0:r0 + c] * jnp.exp2(tot2 - c2)).astype(BF16)
        vb = v_ref[r0:r0 + c].astype(BF16)
        m_s[j] = lax.dot_general(vb, kdec, TN_DIMS, preferred_element_type=F32)

    st = st_ref[...]
    for j in order:
        r0 = j * c
        c2 = c2s[j]
        tot2 = c2[0:1] if rev else c2[c - 1:c]
        qh = (q_s[r0:r0 + c] * jnp.exp2(c2)).astype(BF16)
        o_s[r0:r0 + c, :] = lax.dot_general(qh, st.astype(BF16), NT_DIMS,
                                            preferred_element_type=F32)
        st = jnp.exp2(tot2) * st + m_s[j]
    st_ref[...] = st

    in_sub = jnp.right_shift(row_cc, SUB.bit_length() - 1) == jnp.right_shift(col_cc, SUB.bit_length() - 1)
    for j in order:
        r0 = j * c
        c2 = c2s[j]
        refs = []
        for lo in range(0, c, SUB):
            row = (lo + SUB - 1) if rev else lo
            refs.append(jnp.broadcast_to(c2[row:row + 1], (SUB, dk)))
        d = c2 - jnp.concatenate(refs, axis=0)
        qt = (q_s[r0:r0 + c] * jnp.exp2(d)).astype(BF16)
        kt = (k_s[r0:r0 + c] * jnp.exp2(-d)).astype(BF16)
        a_in = lax.dot_general(qt, kt, NT_DIMS, preferred_element_type=F32)
        a_s[r0:r0 + c, :] = a_offs[j] + jnp.where(in_sub & tri, a_in, 0.0)

    @pl.when(worst_bits > FAST_STEP_BITS)
    def _():
        for j in order:
            r0 = j * c
            c2 = c2s[j]
            qc = q_s[r0:r0 + c]
            blocks = []
            for j1 in range(n_sub):
                lo = j1 * SUB
                qb = qc[lo:lo + SUB]
                cb = c2[lo:lo + SUB]
                blk = jnp.zeros((SUB, c), F32)
                for sl in range(SUB):
                    s = lo + sl
                    cs = cum_s[r0 + s:r0 + s + 1, :]
                    ks = k_s[r0 + s:r0 + s + 1, :]
                    valid_t = (t_loc <= sl) if rev else (t_loc >= sl)
                    p = qb * ks * jnp.exp2(jnp.where(valid_t, cb - cs, -jnp.inf))
                    col = jnp.sum(p, axis=1, keepdims=True)
                    blk = jnp.where(lane_sc == s, col, blk)
                blocks.append(blk)
            a_s[r0:r0 + c, :] = jnp.concatenate(blocks, axis=0) + a_offs[j]

    for j in order:
        r0 = j * c
        vb = v_ref[r0:r0 + c].astype(BF16)
        o_s[r0:r0 + c, :] += jnp.dot(a_s[r0:r0 + c, :].astype(BF16), vb,
                                     preferred_element_type=F32)

    o = o_s[...]
    if rev:
        t = prev_ref[...] + o
        ms = jnp.mean(t * t, axis=-1, keepdims=True)
        y = t * lax.rsqrt(ms + EPS) * nw_ref[...]
        og = og_ref[...]
        o_ref[...] = (y * (og * jax.nn.sigmoid(og))).astype(o_ref.dtype)
    else:
        o_ref[...] = o


def scan(kind, rev, srcs, extras, heads, dk, dv, tb=1024):
    n = srcs["q"][0].shape[0]
    tb = min(tb, n)
    nb = n // tb

    def blk(c):
        return (nb - 1 - c) if rev else c

    def col_spec(width, name):
        base = srcs[name][1] // width
        return pl.BlockSpec((tb, width), lambda h, c: (blk(c), base + h))

    if kind == "gla":
        low, u, bias = extras["low"], extras["u"], extras["bias"]
        operands = [srcs["q"][0], srcs["k"][0], srcs["v"][0], low, u, bias]
        in_specs = [col_spec(dk, "q"), col_spec(dk, "k"), col_spec(dv, "v"),
                    pl.BlockSpec((tb, LANES), lambda h, c: (blk(c), 0)),
                    pl.BlockSpec((LANES, dk), lambda h, c: (0, h)),
                    pl.BlockSpec((1, dk), lambda h, c: (0, h))]
    else:
        lbl = extras["lb_logits"]
        operands = [srcs["q"][0], srcs["f"][0], srcs["v"][0], lbl]
        in_specs = [col_spec(dk, "q"), col_spec(dk, "f"), col_spec(dv, "v"),
                    pl.BlockSpec((lbl.shape[0], dk), lambda h, c: (0, h))]
    if rev:
        operands += [extras["prev"], srcs["og"][0], extras["norm_w"]]
        in_specs += [pl.BlockSpec((tb, dv), lambda h, c: (blk(c), h)),
                     col_spec(dv, "og"),
                     pl.BlockSpec((1, dv), lambda h, c: (0, 0))]
    out_dtype = BF16 if rev else F32
    kern = functools.partial(_scan_kernel, kind=kind, rev=rev, tb=tb, dk=dk, dv=dv)
    return pl.pallas_call(
        kern,
        grid=(heads, nb),
        in_specs=in_specs,
        out_specs=pl.BlockSpec((tb, dv), lambda h, c: (blk(c), h)),
        out_shape=jax.ShapeDtypeStruct((n, heads * dv), out_dtype),
        scratch_shapes=[pltpu.VMEM((dv, dk), F32),
                        pltpu.VMEM((tb, dk), F32),
                        pltpu.VMEM((tb, dk), F32),
                        pltpu.VMEM((tb, dk), F32),
                        pltpu.VMEM((tb, CHUNK), F32),
                        pltpu.VMEM((tb // CHUNK, dv, dk), F32),
                        pltpu.VMEM((tb, dv), F32)],
        compiler_params=_params("parallel", "arbitrary"),
        name=f"scan_{kind}_{'bwd' if rev else 'fwd'}",
    )(*operands)


def _outproj_kernel(a_ref, b_ref, wa_ref, wb_ref, x_ref, o_ref, wa_s, wb_s):
    @pl.when(pl.program_id(1) == 0)
    def _():
        wa_s[...] = wa_ref[...].astype(BF16)
        wb_s[...] = wb_ref[...].astype(BF16)

    acc = jnp.dot(a_ref[...], wa_s[...], preferred_element_type=F32)
    acc += jnp.dot(b_ref[...], wb_s[...], preferred_element_type=F32)
    o_ref[...] = x_ref[...] + acc


def outproj(a, b, w, x, bm=1024, bn=512):
    m, ka = a.shape
    kb = b.shape[1]
    n = w.shape[1]
    assert ka == kb
    bm, bn = min(bm, m), min(bn, n)
    return pl.pallas_call(
        _outproj_kernel,
        grid=(n // bn, m // bm),
        in_specs=[pl.BlockSpec((bm, ka), lambda j, i: (i, 0)),
                  pl.BlockSpec((bm, kb), lambda j, i: (i, 0)),
                  pl.BlockSpec((ka, bn), lambda j, i: (0, j)),
                  pl.BlockSpec((kb, bn), lambda j, i: (1, j)),
                  pl.BlockSpec((bm, bn), lambda j, i: (i, j))],
        out_specs=pl.BlockSpec((bm, bn), lambda j, i: (i, j)),
        out_shape=jax.ShapeDtypeStruct((m, n), F32),
        scratch_shapes=[pltpu.VMEM((ka, bn), BF16), pltpu.VMEM((kb, bn), BF16)],
        compiler_params=_params("parallel", "arbitrary"),
        name="outproj",
    )(a, b, w, w, x)


def _norm_router_kernel(h_ref, w_ref, wr_ref, hn_ref, aff_ref):
    x = h_ref[...]
    ms = jnp.mean(x * x, axis=-1, keepdims=True)
    hn = x * lax.rsqrt(ms + EPS) * w_ref[...]
    d2 = hn.shape[1] // 2
    hn_ref[...] = pltpu.bitcast(
        pltpu.pack_elementwise([hn[:, :d2], hn[:, d2:]], packed_dtype=BF16), U32)
    hn_hi = hn.astype(BF16)
    hn_lo = (hn - hn_hi.astype(F32)).astype(BF16)
    wr = wr_ref[...]
    wr_hi = wr.astype(BF16)
    wr_lo = (wr - wr_hi.astype(F32)).astype(BF16)
    logits = (lax.dot_general(wr_hi, hn_hi, NT_DIMS, preferred_element_type=F32)
              + lax.dot_general(wr_hi, hn_lo, NT_DIMS, preferred_element_type=F32)
              + lax.dot_general(wr_lo, hn_hi, NT_DIMS, preferred_element_type=F32))
    m = jnp.max(logits, axis=0, keepdims=True)
    p = jnp.exp(logits - m)
    aff_ref[...] = p / jnp.sum(p, axis=0, keepdims=True)


def norm_router(h, w, w_router_t, tm=256):
    n, d = h.shape
    e = w_router_t.shape[0]
    return pl.pallas_call(
        _norm_router_kernel,
        grid=(n // tm,),
        in_specs=[pl.BlockSpec((tm, d), lambda i: (i, 0)),
                  pl.BlockSpec((1, d), lambda i: (0, 0)),
                  pl.BlockSpec((e, d), lambda i: (0, 0))],
        out_specs=[pl.BlockSpec((tm, d // 2), lambda i: (i, 0)),
                   pl.BlockSpec((e, tm), lambda i: (0, i))],
        out_shape=[jax.ShapeDtypeStruct((n, d // 2), U32),
                   jax.ShapeDtypeStruct((e, n), F32)],
        compiler_params=_params("parallel"),
        name="norm_router",
    )(h, w.reshape(1, d), w_router_t)


def _prefix_excl(m_f, out_ref, tri_b):
    e, n = m_f.shape
    off = jnp.zeros((e, 1), F32)
    for b in range(n // LANES):
        mb = m_f[:, b * LANES:(b + 1) * LANES]
        inc = jnp.dot(mb.astype(BF16), tri_b, preferred_element_type=F32)
        out_ref[:, b * LANES:(b + 1) * LANES] = inc - mb + off
        off = off + inc[:, LANES - 1:LANES]


def _select_kernel(aff_ref, selrank_ref, cnt_ref, eqr_s, *, cap):
    a = aff_ref[...]
    e, n = a.shape
    bits = jnp.zeros((e, 1), I32)
    for b in range(30, -1, -1):
        cand = bits | (1 << b)
        cnt = jnp.sum(jnp.where(a >= pltpu.bitcast(cand, F32), 1.0, 0.0), axis=1, keepdims=True)
        bits = jnp.where(cnt >= cap, cand, bits)
    thr = pltpu.bitcast(bits, F32)
    gt = a > thr
    eq = a == thr
    n_gt = jnp.sum(jnp.where(gt, 1.0, 0.0), axis=1, keepdims=True)
    need = cap - n_gt
    r = lax.broadcasted_iota(I32, (LANES, LANES), 0)
    cidx = lax.broadcasted_iota(I32, (LANES, LANES), 1)
    tri_b = jnp.where(r <= cidx, 1.0, 0.0).astype(BF16)
    _prefix_excl(jnp.where(eq, 1.0, 0.0), eqr_s, tri_b)
    sel = jnp.where(gt, 1.0, jnp.where(eq, jnp.where(eqr_s[...] < need, 1.0, 0.0), 0.0))
    _prefix_excl(sel, eqr_s, tri_b)
    cnt = eqr_s[...].astype(I32)
    cnt_ref[...] = cnt
    selrank_ref[...] = jnp.where(sel > 0.5, cnt, -1)


def select_topk(aff_t, cap):
    e, n = aff_t.shape
    return pl.pallas_call(
        functools.partial(_select_kernel, cap=cap),
        out_shape=[jax.ShapeDtypeStruct((e, n), I32), jax.ShapeDtypeStruct((e, n), I32)],
        scratch_shapes=[pltpu.VMEM((e, n), F32)],
        compiler_params=pltpu.CompilerParams(vmem_limit_bytes=VMEM_LIMIT),
        name="select_topk",
    )(aff_t)


def _compact_kernel(lo_ref, sr_ref, aff_ref, idx_ref, gate_ref, acc_s, *, cap, tc, rt):
    n = sr_ref.shape[-1]
    n_c = n // tc
    e = pl.program_id(0)
    acc_s[...] = jnp.zeros_like(acc_s)
    rows8 = lax.broadcasted_iota(I32, (8, tc), 0)
    lane8 = lax.broadcasted_iota(I32, (8, tc), 1)
    r_iota = lax.broadcasted_iota(I32, (rt, tc), 0)

    def body(it, carry):
        ci = it // (cap // rt)
        ri = it % (cap // rt)
        c0 = pl.multiple_of(ci * tc, tc)
        q0 = pl.multiple_of(ri * rt, rt)
        lo = lo_ref[e * (n_c + 1) + ci]
        hi = lo_ref[e * (n_c + 1) + ci + 1]

        @pl.when((hi > q0) & (lo < q0 + rt))
        def _():
            sr = sr_ref[0, :, pl.ds(c0, tc)]
            a = aff_ref[0, :, pl.ds(c0, tc)]
            onehot = jnp.where(sr == r_iota + q0, 1.0, 0.0).astype(BF16)
            t = lane8 + c0
            a_hi = a.astype(BF16)
            res = a - a_hi.astype(F32)
            a_mid = res.astype(BF16)
            a_lo = (res - a_mid.astype(F32)).astype(BF16)
            lhs = jnp.where(rows8 == 0, jnp.right_shift(t, 7).astype(F32),
                  jnp.where(rows8 == 1, jnp.bitwise_and(t, LANES - 1).astype(F32),
                  jnp.where(rows8 == 2, a_hi.astype(F32),
                  jnp.where(rows8 == 3, a_mid.astype(F32),
                  jnp.where(rows8 == 4, a_lo.astype(F32), 0.0))))).astype(BF16)
            part = lax.dot_general(lhs, onehot, NT_DIMS, preferred_element_type=F32)
            acc_s[:, pl.ds(q0, rt)] += part

        return carry

    lax.fori_loop(0, n_c * (cap // rt), body, 0)
    acc = acc_s[...]
    idx_ref[0] = (acc[0:1] * float(LANES) + acc[1:2]).astype(I32)
    gate_ref[0] = acc[2:3] + acc[3:4] + acc[4:5]


def compact(selrank, cnt, aff_t, cap, tc=1024, rt=256):
    e, n = selrank.shape
    tc = min(tc, n)
    rt = min(rt, cap)
    sr3 = selrank.reshape(e, 1, n)
    af3 = aff_t.reshape(e, 1, n)
    lo_tab = jnp.concatenate([cnt[:, ::tc], jnp.full((e, 1), cap, I32)], axis=1)
    grid_spec = pltpu.PrefetchScalarGridSpec(
        num_scalar_prefetch=1,
        grid=(e,),
        in_specs=[pl.BlockSpec((1, 1, n), lambda i, lo: (i, 0, 0)),
                  pl.BlockSpec((1, 1, n), lambda i, lo: (i, 0, 0))],
        out_specs=[pl.BlockSpec((1, 1, cap), lambda i, lo: (i, 0, 0)),
                   pl.BlockSpec((1, 1, cap), lambda i, lo: (i, 0, 0))],
        scratch_shapes=[pltpu.VMEM((8, cap), F32)],
    )
    idx, gate = pl.pallas_call(
        functools.partial(_compact_kernel, cap=cap, tc=tc, rt=rt),
        grid_spec=grid_spec,
        out_shape=[jax.ShapeDtypeStruct((e, 1, cap), I32),
                   jax.ShapeDtypeStruct((e, 1, cap), F32)],
        compiler_params=_params("arbitrary"),
        name="compact",
    )(lo_tab.reshape(-1), sr3, af3)
    return idx.reshape(e, cap), gate.reshape(e, cap)


def _ffn_kernel(idx_ref, hp_hbm, gate_ref, w1_ref, w3_ref, w2_ref, y_ref,
                xg_s, xb_s, hid_s, sem, *, cap, nf, bf, n_exp):
    e = pl.program_id(0)
    s = pl.program_id(1)
    d2 = xg_s.shape[1]

    def row_copy(row, r):
        return pltpu.make_async_copy(hp_hbm.at[pl.ds(row, 1), :],
                                     xg_s.at[pl.ds(r, 1), :], sem)

    @pl.when(s == 0)
    def _():
        @pl.when(e == 0)
        def _():
            def issue(r, carry):
                row_copy(idx_ref[r], r).start()
                return carry

            lax.fori_loop(0, cap, issue, 0, unroll=8)

        pltpu.make_async_copy(hp_hbm.at[pl.ds(0, cap), :], xg_s, sem).wait()
        xg = xg_s[...]
        xb_s[:, :d2] = pltpu.unpack_elementwise(
            xg, index=0, packed_dtype=BF16, unpacked_dtype=F32).astype(BF16)
        xb_s[:, d2:] = pltpu.unpack_elementwise(
            xg, index=1, packed_dtype=BF16, unpacked_dtype=F32).astype(BF16)

    @pl.when(s < nf)
    def _():
        nxt = jnp.minimum(e + 1, n_exp - 1)
        share = cap // nf
        for r in range(share):
            slot = s * share + r
            row_copy(idx_ref[nxt * cap + slot], slot).start()

        x = xb_s[...]
        a = jnp.dot(x, w1_ref[0].astype(BF16), preferred_element_type=F32)
        b = jnp.dot(x, w3_ref[0].astype(BF16), preferred_element_type=F32)
        col = pl.multiple_of(s * bf, bf)
        hid_s[:, pl.ds(col, bf)] = (a * jax.nn.sigmoid(a) * b).astype(BF16)

    @pl.when(s >= nf)
    def _():
        y = jnp.dot(hid_s[...], w2_ref[0].astype(BF16), preferred_element_type=F32)
        y_ref[0] = (y * gate_ref[0]).astype(y_ref.dtype)

    @pl.when((e == n_exp - 1) & (s == pl.num_programs(1) - 1))
    def _():
        pltpu.make_async_copy(hp_hbm.at[pl.ds(0, cap), :], xg_s, sem).wait()


def expert_ffn(idx, gate, hn_packed, w1, w3, w2, bf=256, bn=512):
    e, cap = idx.shape
    d2 = hn_packed.shape[1]
    d = 2 * d2
    f = w1.shape[2]
    bf = min(bf, f)
    bn = min(bn, d)
    nf = f // bf
    nn = d // bn
    grid_spec = pltpu.PrefetchScalarGridSpec(
        num_scalar_prefetch=1,
        grid=(e, nf + nn),
        in_specs=[pl.BlockSpec(memory_space=pl.ANY),
                  pl.BlockSpec((1, cap, 1), lambda i, s, idx: (i, 0, 0)),
                  pl.BlockSpec((1, d, bf), lambda i, s, idx: (i, 0, jnp.minimum(s, nf - 1))),
                  pl.BlockSpec((1, d, bf), lambda i, s, idx: (i, 0, jnp.minimum(s, nf - 1))),
                  pl.BlockSpec((1, f, bn), lambda i, s, idx: (i, 0, jnp.maximum(s - nf, 0)))],
        out_specs=pl.BlockSpec((1, cap, bn), lambda i, s, idx: (i, 0, jnp.maximum(s - nf, 0))),
        scratch_shapes=[pltpu.VMEM((cap, d2), U32),
                        pltpu.VMEM((cap, d), BF16),
                        pltpu.VMEM((cap, f), BF16),
                        pltpu.SemaphoreType.DMA(())],
    )
    return pl.pallas_call(
        functools.partial(_ffn_kernel, cap=cap, nf=nf, bf=bf, n_exp=e),
        grid_spec=grid_spec,
        out_shape=jax.ShapeDtypeStruct((e, cap, d), BF16),
        compiler_params=_params("arbitrary", "arbitrary"),
        name="expert_ffn",
    )(idx.reshape(-1), hn_packed, gate.reshape(e, cap, 1), w1, w3, w2)


def _combine_kernel(lo_ref, h_ref, sr_ref, y_hbm, nw_ref, o_ref,
                    acc_s, pack_s, big_s, sem_pack, sem_big,
                    *, n_exp, n_tiles, cap, ws):
    i = pl.program_id(0)
    tt = h_ref.shape[0]
    n_pack = n_exp * ws

    def bounds(tile, e):
        lo = lo_ref[e * (n_tiles + 1) + tile]
        hi = lo_ref[e * (n_tiles + 1) + tile + 1]
        start = jnp.minimum((lo // BF16_SUBLANES) * BF16_SUBLANES, cap - ws)
        return lo, hi, start

    def pack_copy(tile, e, slot):
        _, _, start = bounds(tile, e)
        start = pl.multiple_of(start, BF16_SUBLANES)
        return pltpu.make_async_copy(y_hbm.at[e, pl.ds(start, ws), :],
                                     pack_s.at[slot, pl.ds(e * ws, ws), :],
                                     sem_pack.at[slot])

    def issue(tile, slot):
        for e in range(n_exp):
            pack_copy(tile, e, slot).start()

    @pl.when(i == 0)
    def _():
        issue(0, 0)

    @pl.when(i + 1 < n_tiles)
    def _():
        issue(i + 1, (i + 1) % 2)

    slot = i % 2
    for e in range(n_exp):
        pack_copy(i, e, slot).wait()

    lane = lax.broadcasted_iota(I32, (1, n_pack), 1)
    lane_t = lax.broadcasted_iota(I32, (tt, n_pack), 1)
    target = jnp.full((1, n_pack), -2, I32)
    sr_wide = jnp.full((tt, n_pack), -1, I32)
    for e in range(n_exp):
        lo, hi, start = bounds(i, e)
        covered = hi <= start + ws
        first = jnp.where(covered, start, -(cap + n_pack))
        in_e = (lane >= e * ws) & (lane < (e + 1) * ws)
        target = jnp.where(in_e, first + lane - e * ws, target)
        in_e_t = (lane_t >= e * ws) & (lane_t < (e + 1) * ws)
        sr_wide = jnp.where(in_e_t, sr_ref[:, e:e + 1], sr_wide)
    hit = jnp.where(sr_wide == target, 1.0, 0.0).astype(BF16)
    acc_s[...] = h_ref[...] + jnp.dot(hit, pack_s[slot], preferred_element_type=F32)

    wl = big_s.shape[0]
    j_big = lax.broadcasted_iota(I32, (tt, wl), 1)
    for e in range(n_exp):
        lo, hi, start = bounds(i, e)
        start_l = jnp.minimum(start, cap - wl)
        sr = sr_ref[:, e:e + 1]

        def window(begin, first_slot, e=e, sr=sr):
            begin = pl.multiple_of(begin, BF16_SUBLANES)
            cp = pltpu.make_async_copy(y_hbm.at[e, pl.ds(begin, wl), :], big_s, sem_big)
            cp.start()
            cp.wait()
            slot_id = j_big + begin
            hit_l = jnp.where(sr == slot_id, jnp.where(slot_id >= first_slot, 1.0, 0.0), 0.0)
            acc_s[...] += jnp.dot(hit_l.astype(BF16), big_s[...], preferred_element_type=F32)

        @pl.when(hi > start + ws)
        def _():
            window(start_l, lo)

        @pl.when(hi > start_l + wl)
        def _():
            window(jnp.minimum(start_l + wl, cap - wl), start_l + wl)

    x = acc_s[...]
    ms = jnp.mean(x * x, axis=-1, keepdims=True)
    o_ref[...] = x * lax.rsqrt(ms + EPS) * nw_ref[...]


def combine(h, selrank_t, lo_tab, y, norm_w, tt=256, ws=64):
    n, d = h.shape
    e, cap, _ = y.shape
    tt = min(tt, cap, n)
    ws = min(ws, tt)
    n_tiles = n // tt
    grid_spec = pltpu.PrefetchScalarGridSpec(
        num_scalar_prefetch=1,
        grid=(n_tiles,),
        in_specs=[pl.BlockSpec((tt, d), lambda i, lo: (i, 0)),
                  pl.BlockSpec((tt, e), lambda i, lo: (i, 0)),
                  pl.BlockSpec(memory_space=pl.ANY),
                  pl.BlockSpec((1, d), lambda i, lo: (0, 0))],
        out_specs=pl.BlockSpec((tt, d), lambda i, lo: (i, 0)),
        scratch_shapes=[pltpu.VMEM((tt, d), F32),
                        pltpu.VMEM((2, e * ws, d), BF16),
                        pltpu.VMEM((tt, d), BF16),
                        pltpu.SemaphoreType.DMA((2,)),
                        pltpu.SemaphoreType.DMA(())],
    )
    return pl.pallas_call(
        functools.partial(_combine_kernel, n_exp=e, n_tiles=n_tiles, cap=cap, ws=ws),
        grid_spec=grid_spec,
        out_shape=jax.ShapeDtypeStruct((n, d), F32),
        compiler_params=_params("arbitrary"),
        name="combine",
    )(lo_tab.reshape(-1), h, selrank_t, y, norm_w.reshape(1, d))


def moe_block(h, norm_w, w_router, w1, w3, w2, norm_final_w):
    n, _ = h.shape
    e = w_router.shape[1]
    cap = CAPACITY_FACTOR * n // e
    hn_packed, aff_t = norm_router(h, norm_w, w_router.T)
    selrank, cnt = select_topk(aff_t, cap)
    idx, gate = compact(selrank, cnt, aff_t, cap)
    y = expert_ffn(idx, gate, hn_packed, w1, w3, w2)
    tt = min(256, cap, n)
    lo_tab = jnp.concatenate([cnt[:, ::tt], jnp.full((e, 1), cap, I32)], axis=1)
    return combine(h, selrank.T, lo_tab, y, norm_final_w, tt=tt)


def mixer_block(x, norm_w, w_in, up_f, bias_f, up_b, bias_b, gla_norm_w,
                lb_logits_f, lb_logits_b, hgrn_norm_w, w_out):
    gk = GLA_HEADS * GLA_DK
    gv = GLA_HEADS * GLA_DV
    hd = HGRN_HEADS * HGRN_D
    r = GLA_GATE_RANK
    n_a = 2 * gk + gv
    o_b = n_a + 2 * r
    n_b = w_in.shape[1] - o_b
    wt = jnp.swapaxes(w_in, 0, 1)

    xn = rmsnorm(x, norm_w, BF16)
    proj_a = matmul_nt(xn, wt, 0, n_a, 1024, 512, "inproj_a")
    proj_b = matmul_nt(xn, wt, o_b, n_b, 1024, 512, "inproj_b")
    low = matmul_nt(xn, wt, n_a, LANES, 1024, LANES, "inproj_gate")

    src_gla = {"q": (proj_a, 0), "k": (proj_a, gk), "v": (proj_a, 2 * gk), "og": (proj_b, 0)}
    src_hg = {"q": (proj_b, gv), "v": (proj_b, gv + 3 * hd), "og": (proj_b, gv + 4 * hd)}
    u_f = jnp.zeros((LANES, gk), F32).at[0:r].set(up_f).astype(BF16)
    u_b = jnp.zeros((LANES, gk), F32).at[r:2 * r].set(up_b).astype(BF16)
    gla_f = scan("gla", False, src_gla, {"low": low, "u": u_f, "bias": bias_f.reshape(1, gk)},
                 GLA_HEADS, GLA_DK, GLA_DV)
    gla_o = scan("gla", True, src_gla,
                 {"low": low, "u": u_b, "bias": bias_b.reshape(1, gk), "prev": gla_f,
                  "norm_w": gla_norm_w.reshape(1, GLA_DV)},
                 GLA_HEADS, GLA_DK, GLA_DV)
    hg_f = scan("hgrn", False, dict(src_hg, f=(proj_b, gv + hd)), {"lb_logits": lb_logits_f},
                HGRN_HEADS, HGRN_D, HGRN_D)
    hg_o = scan("hgrn", True, dict(src_hg, f=(proj_b, gv + 2 * hd)),
                {"lb_logits": lb_logits_b, "prev": hg_f,
                 "norm_w": hgrn_norm_w.reshape(1, HGRN_D)},
                HGRN_HEADS, HGRN_D, HGRN_D)
    return outproj(gla_o, hg_o, w_out, x)


def kernel(x, norm_mix_w, w_in, gla_gate_up_f, gla_gate_bias_f, gla_gate_up_b, gla_gate_bias_b, gla_norm_w, hgrn_lb_logits_f, hgrn_lb_logits_b, hgrn_norm_w, w_out, norm_ffn_w, w_router, expert_w1, expert_w3, expert_w2, norm_final_w):
    b, l, d = x.shape
    outs = []
    for bi in range(b):
        xb = x.reshape(l, d) if b == 1 else x[bi]
        h = mixer_block(xb, norm_mix_w[0], w_in[0], gla_gate_up_f[0], gla_gate_bias_f[0],
                        gla_gate_up_b[0], gla_gate_bias_b[0], gla_norm_w[0],
                        hgrn_lb_logits_f, hgrn_lb_logits_b, hgrn_norm_w[0], w_out[0])
        outs.append(moe_block(h, norm_ffn_w[0], w_router[0], expert_w1[0], expert_w3[0],
                              expert_w2[0], norm_final_w))
    return outs[0].reshape(1, l, d) if b == 1 else jnp.stack(outs, axis=0)
```

```python
import functools

import jax
import jax.numpy as jnp
from jax import lax
from jax.experimental import pallas as pl
from jax.experimental.pallas import tpu as pltpu

F32 = jnp.float32
BF16 = jnp.bfloat16
I32 = jnp.int32
U32 = jnp.uint32

EPS = 1e-6
LANES = 128
F32_SUBLANES = 8
BF16_SUBLANES = 16
VMEM_LIMIT = 56 * 1024 * 1024

GLA_HEADS = 8
GLA_DK = 128
GLA_DV = 256
GLA_GATE_RANK = 16
GLA_GATE_NORMALIZER = 16.0
HGRN_HEADS = 16
HGRN_D = 128
CHUNK = 64
SUB = 8
LOG2E = 1.4426950408889634
FAST_STEP_BITS = 12.0
N_EXPERTS = 16
CAPACITY_FACTOR = 2

NT_DIMS = (((1,), (1,)), ((), ()))
TN_DIMS = (((0,), (0,)), ((), ()))


def _params(*sem):
    return pltpu.CompilerParams(dimension_semantics=sem, vmem_limit_bytes=VMEM_LIMIT)


def _rmsnorm_kernel(x_ref, w_ref, o_ref):
    x = x_ref[...]
    ms = jnp.mean(x * x, axis=-1, keepdims=True)
    o_ref[...] = (x * lax.rsqrt(ms + EPS) * w_ref[...]).astype(o_ref.dtype)


def rmsnorm(x, w, out_dtype, tm=256):
    n, d = x.shape
    return pl.pallas_call(
        _rmsnorm_kernel,
        grid=(n // tm,),
        in_specs=[pl.BlockSpec((tm, d), lambda i: (i, 0)),
                  pl.BlockSpec((1, d), lambda i: (0, 0))],
        out_specs=pl.BlockSpec((tm, d), lambda i: (i, 0)),
        out_shape=jax.ShapeDtypeStruct((n, d), out_dtype),
        compiler_params=_params("parallel"),
        name="rmsnorm",
    )(x, w.reshape(1, d))


def _mm_nt_kernel(a_ref, w_ref, o_ref, wb_s):
    @pl.when(pl.program_id(1) == 0)
    def _():
        wb_s[...] = w_ref[...].astype(BF16)

    o_ref[...] = lax.dot_general(a_ref[...], wb_s[...], NT_DIMS, preferred_element_type=F32)


def matmul_nt(a, wt, row0, n, bm, bn, name):
    m, k = a.shape
    bm, bn = min(bm, m), min(bn, n)
    if row0 % bn == 0:
        w_spec = pl.BlockSpec((bn, k), lambda j, i: (row0 // bn + j, 0))
    else:
        assert row0 % F32_SUBLANES == 0 and bn % F32_SUBLANES == 0
        w_spec = pl.BlockSpec(
            (pl.Element(bn), pl.Element(k)),
            lambda j, i: ((row0 // F32_SUBLANES + j * (bn // F32_SUBLANES)) * F32_SUBLANES, 0))
    return pl.pallas_call(
        _mm_nt_kernel,
        grid=(n // bn, m // bm),
        in_specs=[pl.BlockSpec((bm, k), lambda j, i: (i, 0)), w_spec],
        out_specs=pl.BlockSpec((bm, bn), lambda j, i: (i, j)),
        out_shape=jax.ShapeDtypeStruct((m, n), F32),
        scratch_shapes=[pltpu.VMEM((bn, k), BF16)],
        compiler_params=_params("parallel", "arbitrary"),
        name=name,
    )(a, wt)


def _scan_kernel(*refs, kind, rev, tb, dk, dv):
    if kind == "gla":
        q_ref, k_ref, v_ref, low_ref, u_ref, b_ref = refs[:6]
        rest = refs[6:]
    else:
        q_ref, f_ref, v_ref, lbl_ref = refs[:4]
        rest = refs[4:]
    if rev:
        prev_ref, og_ref, nw_ref, o_ref = rest[:4]
        scr = rest[4:]
    else:
        o_ref = rest[0]
        scr = rest[1:]
    st_ref, q_s, k_s, cum_s, a_s, m_s, o_s = scr

    @pl.when(pl.program_id(1) == 0)
    def _():
        st_ref[...] = jnp.zeros_like(st_ref)

    scale = dk ** -0.5
    if kind == "gla":
        z = jnp.dot(low_ref[...].astype(BF16), u_ref[...],
                    preferred_element_type=F32) + b_ref[...]
        log_sig = jnp.minimum(z, 0.0) - jnp.log1p(jnp.exp(-jnp.abs(z)))
        g = log_sig * (1.0 / GLA_GATE_NORMALIZER)
        q_s[...] = q_ref[...] * scale
        k_s[...] = k_ref[...]
    else:
        logits = lbl_ref[...]
        e = jnp.exp(logits - jnp.max(logits, axis=0, keepdims=True))
        lb = e[0:1] / jnp.sum(e, axis=0, keepdims=True)
        f = f_ref[...]
        t = jnp.exp(-jnp.abs(f))
        s_big = 1.0 / (1.0 + t)
        s_small = t * s_big
        k_s[...] = (1.0 - lb) * jnp.where(f >= 0.0, s_small, s_big)
        g = jnp.log(lb + (1.0 - lb) * jnp.where(f >= 0.0, s_big, s_small))
        qq = q_ref[...]
        q_s[...] = qq * jax.nn.sigmoid(qq) * scale
    worst_bits = jnp.max(-g) * LOG2E

    c = CHUNK
    n_sub = c // SUB
    row_cc = lax.broadcasted_iota(I32, (c, c), 0)
    col_cc = lax.broadcasted_iota(I32, (c, c), 1)
    tri = (col_cc >= row_cc) if rev else (col_cc <= row_cc)
    tri_b = jnp.where(tri, 1.0, 0.0).astype(BF16)
    lane_sc = lax.broadcasted_iota(I32, (SUB, c), 1)
    t_loc = lax.broadcasted_iota(I32, (SUB, 1), 0)

    n_chunk = tb // c
    order = list(range(n_chunk - 1, -1, -1) if rev else range(n_chunk))

    c2s = {}
    for j in order:
        r0 = j * c
        gc = g[r0:r0 + c]
        g_hi = gc.astype(BF16)
        res = gc - g_hi.astype(F32)
        g_mid = res.astype(BF16)
        g_lo = (res - g_mid.astype(F32)).astype(BF16)
        cum = (jnp.dot(tri_b, g_hi, preferred_element_type=F32)
               + jnp.dot(tri_b, g_mid, preferred_element_type=F32)
               + jnp.dot(tri_b, g_lo, preferred_element_type=F32))
        c2 = cum * LOG2E
        cum_s[r0:r0 + c, :] = c2
        c2s[j] = c2

    a_offs = {}
    for j in order:
        r0 = j * c
        c2 = c2s[j]
        qc = q_s[r0:r0 + c]
        kc = k_s[r0:r0 + c]
        a_off = None
        w = c // 2
        while w >= SUB:
            refs = []
            for b in range(0, c, 2 * w):
                row = (b + w) if rev else (b + w - 1)
                refs.append(jnp.broadcast_to(c2[row:row + 1], (2 * w, dk)))
            ref_map = refs[0] if len(refs) == 1 else jnp.concatenate(refs, axis=0)
            fac = jnp.exp2(-jnp.abs(c2 - ref_map))
            zero = jnp.zeros((w, dk), F32)
            q_parts, k_parts = [], []
            for b in range(0, c, 2 * w):
                first, second = slice(b, b + w), slice(b + w, b + 2 * w)
                if rev:
                    q_parts += [qc[first] * fac[first], zero]
                    k_parts += [zero, kc[second] * fac[second]]
                else:
                    q_parts += [zero, qc[second] * fac[second]]
                    k_parts += [kc[first] * fac[first], zero]
            qt = jnp.concatenate(q_parts, axis=0).astype(BF16)
            kt = jnp.concatenate(k_parts, axis=0).astype(BF16)
            a_w = lax.dot_general(qt, kt, NT_DIMS, preferred_element_type=F32)
            if 2 * w < c:
                shift = (2 * w).bit_length() - 1
                same = jnp.right_shift(row_cc, shift) == jnp.right_shift(col_cc, shift)
                a_w = jnp.where(same, a_w, 0.0)
            a_off = a_w if a_off is None else a_off + a_w
            w //= 2
        a_offs[j] = a_off

    for j in order:
        r0 = j * c
        c2 = c2s[j]
        tot2 = c2[0:1] if rev else c2[c - 1:c]
        kdec = (k_s[r0:r0 + c] * jnp.exp2(tot2 - c2)).astype(BF16)
        vb = v_ref[r0:r0 + c].astype(BF16)
        m_s[j] = lax.dot_general(vb, kdec, TN_DIMS, preferred_element_type=F32)

    st = st_ref[...]
    for j in order:
        r0 = j * c
        c2 = c2s[j]
        tot2 = c2[0:1] if rev else c2[c - 1:c]
        qh = (q_s[r0:r0 + c] * jnp.exp2(c2)).astype(BF16)
        o_s[r0:r0 + c, :] = lax.dot_general(qh, st.astype(BF16), NT_DIMS,
                                            preferred_element_type=F32)
        st = jnp.exp2(tot2) * st + m_s[j]
    st_ref[...] = st

    in_sub = jnp.right_shift(row_cc, SUB.bit_length() - 1) == jnp.right_shift(col_cc, SUB.bit_length() - 1)
    for j in order:
        r0 = j * c
        c2 = c2s[j]
        refs = []
        for lo in range(0, c, SUB):
            row = (lo + SUB - 1) if rev else lo
            refs.append(jnp.broadcast_to(c2[row:row + 1], (SUB, dk)))
        d = c2 - jnp.concatenate(refs, axis=0)
        qt = (q_s[r0:r0 + c] * jnp.exp2(d)).astype(BF16)
        kt = (k_s[r0:r0 + c] * jnp.exp2(-d)).astype(BF16)
        a_in = lax.dot_general(qt, kt, NT_DIMS, preferred_element_type=F32)
        a_s[r0:r0 + c, :] = a_offs[j] + jnp.where(in_sub & tri, a_in, 0.0)

    @pl.when(worst_bits > FAST_STEP_BITS)
    def _():
        for j in order:
            r0 = j * c
            c2 = c2s[j]
            qc = q_s[r0:r0 + c]
            blocks = []
            for j1 in range(n_sub):
                lo = j1 * SUB
                qb = qc[lo:lo + SUB]
                cb = c2[lo:lo + SUB]
                blk = jnp.zeros((SUB, c), F32)
                for sl in range(SUB):
                    s = lo + sl
                    cs = cum_s[r0 + s:r0 + s + 1, :]
                    ks = k_s[r0 + s:r0 + s + 1, :]
                    valid_t = (t_loc <= sl) if rev else (t_loc >= sl)
                    p = qb * ks * jnp.exp2(jnp.where(valid_t, cb - cs, -jnp.inf))
                    col = jnp.sum(p, axis=1, keepdims=True)
                    blk = jnp.where(lane_sc == s, col, blk)
                blocks.append(blk)
            a_s[r0:r0 + c, :] = jnp.concatenate(blocks, axis=0) + a_offs[j]

    for j in order:
        r0 = j * c
        vb = v_ref[r0:r0 + c].astype(BF16)
        o_s[r0:r0 + c, :] += jnp.dot(a_s[r0:r0 + c, :].astype(BF16), vb,
                                     preferred_element_type=F32)

    o = o_s[...]
    if rev:
        t = prev_ref[...] + o
        ms = jnp.mean(t * t, axis=-1, keepdims=True)
        y = t * lax.rsqrt(ms + EPS) * nw_ref[...]
        og = og_ref[...]
        o_ref[...] = (y * (og * jax.nn.sigmoid(og))).astype(o_ref.dtype)
    else:
        o_ref[...] = o


def scan(kind, rev, srcs, extras, heads, dk, dv, tb=1024):
    n = srcs["q"][0].shape[0]
    tb = min(tb, n)
    nb = n // tb

    def blk(c):
        return (nb - 1 - c) if rev else c

    def col_spec(width, name):
        base = srcs[name][1] // width
        return pl.BlockSpec((tb, width), lambda h, c: (blk(c), base + h))

    if kind == "gla":
        low, u, bias = extras["low"], extras["u"], extras["bias"]
        operands = [srcs["q"][0], srcs["k"][0], srcs["v"][0], low, u, bias]
        in_specs = [col_spec(dk, "q"), col_spec(dk, "k"), col_spec(dv, "v"),
                    pl.BlockSpec((tb, LANES), lambda h, c: (blk(c), 0)),
                    pl.BlockSpec((LANES, dk), lambda h, c: (0, h)),
                    pl.BlockSpec((1, dk), lambda h, c: (0, h))]
    else:
        lbl = extras["lb_logits"]
        operands = [srcs["q"][0], srcs["f"][0], srcs["v"][0], lbl]
        in_specs = [col_spec(dk, "q"), col_spec(dk, "f"), col_spec(dv, "v"),
                    pl.BlockSpec((lbl.shape[0], dk), lambda h, c: (0, h))]
    if rev:
        operands += [extras["prev"], srcs["og"][0], extras["norm_w"]]
        in_specs += [pl.BlockSpec((tb, dv), lambda h, c: (blk(c), h)),
                     col_spec(dv, "og"),
                     pl.BlockSpec((1, dv), lambda h, c: (0, 0))]
    out_dtype = BF16 if rev else F32
    kern = functools.partial(_scan_kernel, kind=kind, rev=rev, tb=tb, dk=dk, dv=dv)
    return pl.pallas_call(
        kern,
        grid=(heads, nb),
        in_specs=in_specs,
        out_specs=pl.BlockSpec((tb, dv), lambda h, c: (blk(c), h)),
        out_shape=jax.ShapeDtypeStruct((n, heads * dv), out_dtype),
        scratch_shapes=[pltpu.VMEM((dv, dk), F32),
                        pltpu.VMEM((tb, dk), F32),
                        pltpu.VMEM((tb, dk), F32),
                        pltpu.VMEM((tb, dk), F32),
                        pltpu.VMEM((tb, CHUNK), F32),
                        pltpu.VMEM((tb // CHUNK, dv, dk), F32),
                        pltpu.VMEM((tb, dv), F32)],
        compiler_params=_params("parallel", "arbitrary"),
        name=f"scan_{kind}_{'bwd' if rev else 'fwd'}",
    )(*operands)


def _outproj_kernel(a_ref, b_ref, wa_ref, wb_ref, x_ref, o_ref, wa_s, wb_s):
    @pl.when(pl.program_id(1) == 0)
    def _():
        wa_s[...] = wa_ref[...].astype(BF16)
        wb_s[...] = wb_ref[...].astype(BF16)

    acc = jnp.dot(a_ref[...], wa_s[...], preferred_element_type=F32)
    acc += jnp.dot(b_ref[...], wb_s[...], preferred_element_type=F32)
    o_ref[...] = x_ref[...] + acc


def outproj(a, b, w, x, bm=1024, bn=512):
    m, ka = a.shape
    kb = b.shape[1]
    n = w.shape[1]
    assert ka == kb
    bm, bn = min(bm, m), min(bn, n)
    return pl.pallas_call(
        _outproj_kernel,
        grid=(n // bn, m // bm),
        in_specs=[pl.BlockSpec((bm, ka), lambda j, i: (i, 0)),
                  pl.BlockSpec((bm, kb), lambda j, i: (i, 0)),
                  pl.BlockSpec((ka, bn), lambda j, i: (0, j)),
                  pl.BlockSpec((kb, bn), lambda j, i: (1, j)),
                  pl.BlockSpec((bm, bn), lambda j, i: (i, j))],
        out_specs=pl.BlockSpec((bm, bn), lambda j, i: (i, j)),
        out_shape=jax.ShapeDtypeStruct((m, n), F32),
        scratch_shapes=[pltpu.VMEM((ka, bn), BF16), pltpu.VMEM((kb, bn), BF16)],
        compiler_params=_params("parallel", "arbitrary"),
        name="outproj",
    )(a, b, w, w, x)


def _norm_router_kernel(h_ref, w_ref, wr_ref, hn_ref, aff_ref):
    x = h_ref[...]
    ms = jnp.mean(x * x, axis=-1, keepdims=True)
    hn = x * lax.rsqrt(ms + EPS) * w_ref[...]
    d2 = hn.shape[1] // 2
    hn_ref[...] = pltpu.bitcast(
        pltpu.pack_elementwise([hn[:, :d2], hn[:, d2:]], packed_dtype=BF16), U32)
    hn_hi = hn.astype(BF16)
    hn_lo = (hn - hn_hi.astype(F32)).astype(BF16)
    wr = wr_ref[...]
    wr_hi = wr.astype(BF16)
    wr_lo = (wr - wr_hi.astype(F32)).astype(BF16)
    logits = (lax.dot_general(wr_hi, hn_hi, NT_DIMS, preferred_element_type=F32)
              + lax.dot_general(wr_hi, hn_lo, NT_DIMS, preferred_element_type=F32)
              + lax.dot_general(wr_lo, hn_hi, NT_DIMS, preferred_element_type=F32))
    m = jnp.max(logits, axis=0, keepdims=True)
    p = jnp.exp(logits - m)
    aff_ref[...] = p / jnp.sum(p, axis=0, keepdims=True)


def norm_router(h, w, w_router_t, tm=256):
    n, d = h.shape
    e = w_router_t.shape[0]
    return pl.pallas_call(
        _norm_router_kernel,
        grid=(n // tm,),
        in_specs=[pl.BlockSpec((tm, d), lambda i: (i, 0)),
                  pl.BlockSpec((1, d), lambda i: (0, 0)),
                  pl.BlockSpec((e, d), lambda i: (0, 0))],
        out_specs=[pl.BlockSpec((tm, d // 2), lambda i: (i, 0)),
                   pl.BlockSpec((e, tm), lambda i: (0, i))],
        out_shape=[jax.ShapeDtypeStruct((n, d // 2), U32),
                   jax.ShapeDtypeStruct((e, n), F32)],
        compiler_params=_params("parallel"),
        name="norm_router",
    )(h, w.reshape(1, d), w_router_t)


def _prefix_excl(m_f, out_ref, tri_b):
    e, n = m_f.shape
    off = jnp.zeros((e, 1), F32)
    for b in range(n // LANES):
        mb = m_f[:, b * LANES:(b + 1) * LANES]
        inc = jnp.dot(mb.astype(BF16), tri_b, preferred_element_type=F32)
        out_ref[:, b * LANES:(b + 1) * LANES] = inc - mb + off
        off = off + inc[:, LANES - 1:LANES]


def _select_kernel(aff_ref, selrank_ref, cnt_ref, eqr_s, *, cap):
    a = aff_ref[...]
    e, n = a.shape
    bits = jnp.zeros((e, 1), I32)
    for b in range(30, -1, -1):
        cand = bits | (1 << b)
        cnt = jnp.sum(jnp.where(a >= pltpu.bitcast(cand, F32), 1.0, 0.0), axis=1, keepdims=True)
        bits = jnp.where(cnt >= cap, cand, bits)
    thr = pltpu.bitcast(bits, F32)
    gt = a > thr
    eq = a == thr
    n_gt = jnp.sum(jnp.where(gt, 1.0, 0.0), axis=1, keepdims=True)
    need = cap - n_gt
    r = lax.broadcasted_iota(I32, (LANES, LANES), 0)
    cidx = lax.broadcasted_iota(I32, (LANES, LANES), 1)
    tri_b = jnp.where(r <= cidx, 1.0, 0.0).astype(BF16)
    _prefix_excl(jnp.where(eq, 1.0, 0.0), eqr_s, tri_b)
    sel = jnp.where(gt, 1.0, jnp.where(eq, jnp.where(eqr_s[...] < need, 1.0, 0.0), 0.0))
    _prefix_excl(sel, eqr_s, tri_b)
    cnt = eqr_s[...].astype(I32)
    cnt_ref[...] = cnt
    selrank_ref[...] = jnp.where(sel > 0.5, cnt, -1)


def select_topk(aff_t, cap):
    e, n = aff_t.shape
    return pl.pallas_call(
        functools.partial(_select_kernel, cap=cap),
        out_shape=[jax.ShapeDtypeStruct((e, n), I32), jax.ShapeDtypeStruct((e, n), I32)],
        scratch_shapes=[pltpu.VMEM((e, n), F32)],
        compiler_params=pltpu.CompilerParams(vmem_limit_bytes=VMEM_LIMIT),
        name="select_topk",
    )(aff_t)


def _compact_kernel(lo_ref, sr_ref, aff_ref, idx_ref, gate_ref, acc_s, *, cap, tc, rt):
    n = sr_ref.shape[-1]
    n_c = n // tc
    e = pl.program_id(0)
    acc_s[...] = jnp.zeros_like(acc_s)
    rows8 = lax.broadcasted_iota(I32, (8, tc), 0)
    lane8 = lax.broadcasted_iota(I32, (8, tc), 1)
    r_iota = lax.broadcasted_iota(I32, (rt, tc), 0)

    def body(it, carry):
        ci = it // (cap // rt)
        ri = it % (cap // rt)
        c0 = pl.multiple_of(ci * tc, tc)
        q0 = pl.multiple_of(ri * rt, rt)
        lo = lo_ref[e * (n_c + 1) + ci]
        hi = lo_ref[e * (n_c + 1) + ci + 1]

        @pl.when((hi > q0) & (lo < q0 + rt))
        def _():
            sr = sr_ref[0, :, pl.ds(c0, tc)]
            a = aff_ref[0, :, pl.ds(c0, tc)]
            onehot = jnp.where(sr == r_iota + q0, 1.0, 0.0).astype(BF16)
            t = lane8 + c0
            a_hi = a.astype(BF16)
            res = a - a_hi.astype(F32)
            a_mid = res.astype(BF16)
            a_lo = (res - a_mid.astype(F32)).astype(BF16)
            lhs = jnp.where(rows8 == 0, jnp.right_shift(t, 7).astype(F32),
                  jnp.where(rows8 == 1, jnp.bitwise_and(t, LANES - 1).astype(F32),
                  jnp.where(rows8 == 2, a_hi.astype(F32),
                  jnp.where(rows8 == 3, a_mid.astype(F32),
                  jnp.where(rows8 == 4, a_lo.astype(F32), 0.0))))).astype(BF16)
            part = lax.dot_general(lhs, onehot, NT_DIMS, preferred_element_type=F32)
            acc_s[:, pl.ds(q0, rt)] += part

        return carry

    lax.fori_loop(0, n_c * (cap // rt), body, 0)
    acc = acc_s[...]
    idx_ref[0] = (acc[0:1] * float(LANES) + acc[1:2]).astype(I32)
    gate_ref[0] = acc[2:3] + acc[3:4] + acc[4:5]


def compact(selrank, cnt, aff_t, cap, tc=1024, rt=256):
    e, n = selrank.shape
    tc = min(tc, n)
    rt = min(rt, cap)
    sr3 = selrank.reshape(e, 1, n)
    af3 = aff_t.reshape(e, 1, n)
    lo_tab = jnp.concatenate([cnt[:, ::tc], jnp.full((e, 1), cap, I32)], axis=1)
    grid_spec = pltpu.PrefetchScalarGridSpec(
        num_scalar_prefetch=1,
        grid=(e,),
        in_specs=[pl.BlockSpec((1, 1, n), lambda i, lo: (i, 0, 0)),
                  pl.BlockSpec((1, 1, n), lambda i, lo: (i, 0, 0))],
        out_specs=[pl.BlockSpec((1, 1, cap), lambda i, lo: (i, 0, 0)),
                   pl.BlockSpec((1, 1, cap), lambda i, lo: (i, 0, 0))],
        scratch_shapes=[pltpu.VMEM((8, cap), F32)],
    )
    idx, gate = pl.pallas_call(
        functools.partial(_compact_kernel, cap=cap, tc=tc, rt=rt),
        grid_spec=grid_spec,
        out_shape=[jax.ShapeDtypeStruct((e, 1, cap), I32),
                   jax.ShapeDtypeStruct((e, 1, cap), F32)],
        compiler_params=_params("arbitrary"),
        name="compact",
    )(lo_tab.reshape(-1), sr3, af3)
    return idx.reshape(e, cap), gate.reshape(e, cap)


def _ffn_kernel(idx_ref, hp_hbm, gate_ref, w1_ref, w3_ref, w2_ref, y_ref,
                xg_s, xb_s, hid_s, sem, *, cap, nf, bf, n_exp):
    e = pl.program_id(0)
    s = pl.program_id(1)
    d2 = xg_s.shape[1]

    def row_copy(row, r):
        return pltpu.make_async_copy(hp_hbm.at[pl.ds(row, 1), :],
                                     xg_s.at[pl.ds(r, 1), :], sem)

    @pl.when(s == 0)
    def _():
        @pl.when(e == 0)
        def _():
            def issue(r, carry):
                row_copy(idx_ref[r], r).start()
                return carry

            lax.fori_loop(0, cap, issue, 0, unroll=8)

        pltpu.make_async_copy(hp_hbm.at[pl.ds(0, cap), :], xg_s, sem).wait()
        xg = xg_s[...]
        xb_s[:, :d2] = pltpu.unpack_elementwise(
            xg, index=0, packed_dtype=BF16, unpacked_dtype=F32).astype(BF16)
        xb_s[:, d2:] = pltpu.unpack_elementwise(
            xg, index=1, packed_dtype=BF16, unpacked_dtype=F32).astype(BF16)

    @pl.when(s < nf)
    def _():
        nxt = jnp.minimum(e + 1, n_exp - 1)
        share = cap // nf
        for r in range(share):
            slot = s * share + r
            row_copy(idx_ref[nxt * cap + slot], slot).start()

        x = xb_s[...]
        a = jnp.dot(x, w1_ref[0].astype(BF16), preferred_element_type=F32)
        b = jnp.dot(x, w3_ref[0].astype(BF16), preferred_element_type=F32)
        col = pl.multiple_of(s * bf, bf)
        hid_s[:, pl.ds(col, bf)] = (a * jax.nn.sigmoid(a) * b).astype(BF16)

    @pl.when(s >= nf)
    def _():
        y = jnp.dot(hid_s[...], w2_ref[0].astype(BF16), preferred_element_type=F32)
        y_ref[0] = (y * gate_ref[0]).astype(y_ref.dtype)

    @pl.when((e == n_exp - 1) & (s == pl.num_programs(1) - 1))
    def _():
        pltpu.make_async_copy(hp_hbm.at[pl.ds(0, cap), :], xg_s, sem).wait()


def expert_ffn(idx, gate, hn_packed, w1, w3, w2, bf=256, bn=512):
    e, cap = idx.shape
    d2 = hn_packed.shape[1]
    d = 2 * d2
    f = w1.shape[2]
    bf = min(bf, f)
    bn = min(bn, d)
    nf = f // bf
    nn = d // bn
    grid_spec = pltpu.PrefetchScalarGridSpec(
        num_scalar_prefetch=1,
        grid=(e, nf + nn),
        in_specs=[pl.BlockSpec(memory_space=pl.ANY),
                  pl.BlockSpec((1, cap, 1), lambda i, s, idx: (i, 0, 0)),
                  pl.BlockSpec((1, d, bf), lambda i, s, idx: (i, 0, jnp.minimum(s, nf - 1))),
                  pl.BlockSpec((1, d, bf), lambda i, s, idx: (i, 0, jnp.minimum(s, nf - 1))),
                  pl.BlockSpec((1, f, bn), lambda i, s, idx: (i, 0, jnp.maximum(s - nf, 0)))],
        out_specs=pl.BlockSpec((1, cap, bn), lambda i, s, idx: (i, 0, jnp.maximum(s - nf, 0))),
        scratch_shapes=[pltpu.VMEM((cap, d2), U32),
                        pltpu.VMEM((cap, d), BF16),
                        pltpu.VMEM((cap, f), BF16),
                        pltpu.SemaphoreType.DMA(())],
    )
    return pl.pallas_call(
        functools.partial(_ffn_kernel, cap=cap, nf=nf, bf=bf, n_exp=e),
        grid_spec=grid_spec,
        out_shape=jax.ShapeDtypeStruct((e, cap, d), BF16),
        compiler_params=_params("arbitrary", "arbitrary"),
        name="expert_ffn",
    )(idx.reshape(-1), hn_packed, gate.reshape(e, cap, 1), w1, w3, w2)


def _combine_kernel(lo_ref, h_ref, sr_ref, y_hbm, nw_ref, o_ref,
                    acc_s, pack_s, big_s, sem_pack, sem_big,
                    *, n_exp, n_tiles, cap, ws):
    i = pl.program_id(0)
    tt = h_ref.shape[0]
    n_pack = n_exp * ws

    def bounds(tile, e):
        lo = lo_ref[e * (n_tiles + 1) + tile]
        hi = lo_ref[e * (n_tiles + 1) + tile + 1]
        start = jnp.minimum((lo // BF16_SUBLANES) * BF16_SUBLANES, cap - ws)
        return lo, hi, start

    def pack_copy(tile, e, slot):
        _, _, start = bounds(tile, e)
        start = pl.multiple_of(start, BF16_SUBLANES)
        return pltpu.make_async_copy(y_hbm.at[e, pl.ds(start, ws), :],
                                     pack_s.at[slot, pl.ds(e * ws, ws), :],
                                     sem_pack.at[slot])

    def issue(tile, slot):
        for e in range(n_exp):
            pack_copy(tile, e, slot).start()

    @pl.when(i == 0)
    def _():
        issue(0, 0)

    @pl.when(i + 1 < n_tiles)
    def _():
        issue(i + 1, (i + 1) % 2)

    slot = i % 2
    for e in range(n_exp):
        pack_copy(i, e, slot).wait()

    lane = lax.broadcasted_iota(I32, (1, n_pack), 1)
    target = jnp.full((1, n_pack), -2, I32)
    for e in range(n_exp):
        lo, hi, start = bounds(i, e)
        covered = hi <= start + ws
        first = jnp.where(covered, start, -(cap + n_pack))
        in_e = (lane >= e * ws) & (lane < (e + 1) * ws)
        target = jnp.where(in_e, first + lane - e * ws, target)
    per_tile = LANES // ws
    lane_in = lax.broadcasted_iota(I32, (tt, LANES), 1)
    cols = []
    for p in range(n_pack // LANES):
        e0 = p * per_tile
        blk = jnp.broadcast_to(sr_ref[:, e0:e0 + 1], (tt, LANES))
        for q in range(1, per_tile):
            blk = jnp.where(lane_in >= q * ws, sr_ref[:, e0 + q:e0 + q + 1], blk)
        cols.append(blk)
    sr_wide = jnp.concatenate(cols, axis=1)
    hit = jnp.where(sr_wide == target, 1.0, 0.0).astype(BF16)
    acc_s[...] = h_ref[...] + jnp.dot(hit, pack_s[slot], preferred_element_type=F32)

    wl = big_s.shape[0]
    j_big = lax.broadcasted_iota(I32, (tt, wl), 1)
    for e in range(n_exp):
        lo, hi, start = bounds(i, e)
        start_l = jnp.minimum(start, cap - wl)
        sr = sr_ref[:, e:e + 1]

        def window(begin, first_slot, e=e, sr=sr):
            begin = pl.multiple_of(begin, BF16_SUBLANES)
            cp = pltpu.make_async_copy(y_hbm.at[e, pl.ds(begin, wl), :], big_s, sem_big)
            cp.start()
            cp.wait()
            slot_id = j_big + begin
            hit_l = jnp.where(sr == slot_id, jnp.where(slot_id >= first_slot, 1.0, 0.0), 0.0)
            acc_s[...] += jnp.dot(hit_l.astype(BF16), big_s[...], preferred_element_type=F32)

        @pl.when(hi > start + ws)
        def _():
            window(start_l, lo)

        @pl.when(hi > start_l + wl)
        def _():
            window(jnp.minimum(start_l + wl, cap - wl), start_l + wl)

    x = acc_s[...]
    ms = jnp.mean(x * x, axis=-1, keepdims=True)
    o_ref[...] = x * lax.rsqrt(ms + EPS) * nw_ref[...]


def combine(h, selrank_t, lo_tab, y, norm_w, tt=256, ws=64):
    n, d = h.shape
    e, cap, _ = y.shape
    tt = min(tt, cap, n)
    ws = min(ws, tt)
    assert LANES % ws == 0
    n_tiles = n // tt
    grid_spec = pltpu.PrefetchScalarGridSpec(
        num_scalar_prefetch=1,
        grid=(n_tiles,),
        in_specs=[pl.BlockSpec((tt, d), lambda i, lo: (i, 0)),
                  pl.BlockSpec((tt, e), lambda i, lo: (i, 0)),
                  pl.BlockSpec(memory_space=pl.ANY),
                  pl.BlockSpec((1, d), lambda i, lo: (0, 0))],
        out_specs=pl.BlockSpec((tt, d), lambda i, lo: (i, 0)),
        scratch_shapes=[pltpu.VMEM((tt, d), F32),
                        pltpu.VMEM((2, e * ws, d), BF16),
                        pltpu.VMEM((tt, d), BF16),
                        pltpu.SemaphoreType.DMA((2,)),
                        pltpu.SemaphoreType.DMA(())],
    )
    return pl.pallas_call(
        functools.partial(_combine_kernel, n_exp=e, n_tiles=n_tiles, cap=cap, ws=ws),
        grid_spec=grid_spec,
        out_shape=jax.ShapeDtypeStruct((n, d), F32),
        compiler_params=_params("arbitrary"),
        name="combine",
    )(lo_tab.reshape(-1), h, selrank_t, y, norm_w.reshape(1, d))


def moe_block(h, norm_w, w_router, w1, w3, w2, norm_final_w):
    n, _ = h.shape
    e = w_router.shape[1]
    cap = CAPACITY_FACTOR * n // e
    hn_packed, aff_t = norm_router(h, norm_w, w_router.T)
    selrank, cnt = select_topk(aff_t, cap)
    idx, gate = compact(selrank, cnt, aff_t, cap)
    y = expert_ffn(idx, gate, hn_packed, w1, w3, w2)
    tt = min(256, cap, n)
    lo_tab = jnp.concatenate([cnt[:, ::tt], jnp.full((e, 1), cap, I32)], axis=1)
    return combine(h, selrank.T, lo_tab, y, norm_final_w, tt=tt)


def mixer_block(x, norm_w, w_in, up_f, bias_f, up_b, bias_b, gla_norm_w,
                lb_logits_f, lb_logits_b, hgrn_norm_w, w_out):
    gk = GLA_HEADS * GLA_DK
    gv = GLA_HEADS * GLA_DV
    hd = HGRN_HEADS * HGRN_D
    r = GLA_GATE_RANK
    n_a = 2 * gk + gv
    o_b = n_a + 2 * r
    n_b = w_in.shape[1] - o_b
    wt = jnp.swapaxes(w_in, 0, 1)

    xn = rmsnorm(x, norm_w, BF16)
    proj_a = matmul_nt(xn, wt, 0, n_a, 512, 1024, "inproj_a")
    proj_b = matmul_nt(xn, wt, o_b, n_b, 512, 1024, "inproj_b")
    low = matmul_nt(xn, wt, n_a, LANES, 1024, LANES, "inproj_gate")

    src_gla = {"q": (proj_a, 0), "k": (proj_a, gk), "v": (proj_a, 2 * gk), "og": (proj_b, 0)}
    src_hg = {"q": (proj_b, gv), "v": (proj_b, gv + 3 * hd), "og": (proj_b, gv + 4 * hd)}
    u_f = jnp.zeros((LANES, gk), F32).at[0:r].set(up_f).astype(BF16)
    u_b = jnp.zeros((LANES, gk), F32).at[r:2 * r].set(up_b).astype(BF16)
    gla_f = scan("gla", False, src_gla, {"low": low, "u": u_f, "bias": bias_f.reshape(1, gk)},
                 GLA_HEADS, GLA_DK, GLA_DV)
    gla_o = scan("gla", True, src_gla,
                 {"low": low, "u": u_b, "bias": bias_b.reshape(1, gk), "prev": gla_f,
                  "norm_w": gla_norm_w.reshape(1, GLA_DV)},
                 GLA_HEADS, GLA_DK, GLA_DV)
    hg_f = scan("hgrn", False, dict(src_hg, f=(proj_b, gv + hd)), {"lb_logits": lb_logits_f},
                HGRN_HEADS, HGRN_D, HGRN_D)
    hg_o = scan("hgrn", True, dict(src_hg, f=(proj_b, gv + 2 * hd)),
                {"lb_logits": lb_logits_b, "prev": hg_f,
                 "norm_w": hgrn_norm_w.reshape(1, HGRN_D)},
                HGRN_HEADS, HGRN_D, HGRN_D)
    return outproj(gla_o, hg_o, w_out, x)


def kernel(x, norm_mix_w, w_in, gla_gate_up_f, gla_gate_bias_f, gla_gate_up_b, gla_gate_bias_b, gla_norm_w, hgrn_lb_logits_f, hgrn_lb_logits_b, hgrn_norm_w, w_out, norm_ffn_w, w_router, expert_w1, expert_w3, expert_w2, norm_final_w):
    b, l, d = x.shape
    outs = []
    for bi in range(b):
        xb = x.reshape(l, d) if b == 1 else x[bi]
        h = mixer_block(xb, norm_mix_w[0], w_in[0], gla_gate_up_f[0], gla_gate_bias_f[0],
                        gla_gate_up_b[0], gla_gate_bias_b[0], gla_norm_w[0],
                        hgrn_lb_logits_f, hgrn_lb_logits_b, hgrn_norm_w[0], w_out[0])
        outs.append(moe_block(h, norm_ffn_w[0], w_router[0], expert_w1[0], expert_w3[0],
                              expert_w2[0], norm_final_w))
    return outs[0].reshape(1, l, d) if b == 1 else jnp.stack(outs, axis=0)
```

```python
import functools

import jax
import jax.numpy as jnp
from jax import lax
from jax.experimental import pallas as pl
from jax.experimental.pallas import tpu as pltpu

F32 = jnp.float32
BF16 = jnp.bfloat16
I32 = jnp.int32
U32 = jnp.uint32

EPS = 1e-6
LANES = 128
F32_SUBLANES = 8
BF16_SUBLANES = 16
VMEM_LIMIT = 56 * 1024 * 1024

GLA_HEADS = 8
GLA_DK = 128
GLA_DV = 256
GLA_GATE_RANK = 16
GLA_GATE_NORMALIZER = 16.0
HGRN_HEADS = 16
HGRN_D = 128
CHUNK = 64
SUB = 8
LOG2E = 1.4426950408889634
FAST_STEP_BITS = 12.0
N_EXPERTS = 16
CAPACITY_FACTOR = 2

NT_DIMS = (((1,), (1,)), ((), ()))
TN_DIMS = (((0,), (0,)), ((), ()))


def _params(*sem):
    return pltpu.CompilerParams(dimension_semantics=sem, vmem_limit_bytes=VMEM_LIMIT)


def _rmsnorm_kernel(x_ref, w_ref, o_ref):
    x = x_ref[...]
    ms = jnp.mean(x * x, axis=-1, keepdims=True)
    o_ref[...] = (x * lax.rsqrt(ms + EPS) * w_ref[...]).astype(o_ref.dtype)


def rmsnorm(x, w, out_dtype, tm=256):
    n, d = x.shape
    return pl.pallas_call(
        _rmsnorm_kernel,
        grid=(n // tm,),
        in_specs=[pl.BlockSpec((tm, d), lambda i: (i, 0)),
                  pl.BlockSpec((1, d), lambda i: (0, 0))],
        out_specs=pl.BlockSpec((tm, d), lambda i: (i, 0)),
        out_shape=jax.ShapeDtypeStruct((n, d), out_dtype),
        compiler_params=_params("parallel"),
        name="rmsnorm",
    )(x, w.reshape(1, d))


def _mm_nt_kernel(a_ref, w_ref, o_ref, wb_s):
    @pl.when(pl.program_id(1) == 0)
    def _():
        wb_s[...] = w_ref[...].T.astype(BF16)

    o_ref[...] = jnp.dot(a_ref[...], wb_s[...], preferred_element_type=F32)


def matmul_nt(a, wt, row0, n, bm, bn, name):
    m, k = a.shape
    bm, bn = min(bm, m), min(bn, n)
    if row0 % bn == 0:
        w_spec = pl.BlockSpec((bn, k), lambda j, i: (row0 // bn + j, 0))
    else:
        assert row0 % F32_SUBLANES == 0 and bn % F32_SUBLANES == 0
        w_spec = pl.BlockSpec(
            (pl.Element(bn), pl.Element(k)),
            lambda j, i: ((row0 // F32_SUBLANES + j * (bn // F32_SUBLANES)) * F32_SUBLANES, 0))
    return pl.pallas_call(
        _mm_nt_kernel,
        grid=(n // bn, m // bm),
        in_specs=[pl.BlockSpec((bm, k), lambda j, i: (i, 0)), w_spec],
        out_specs=pl.BlockSpec((bm, bn), lambda j, i: (i, j)),
        out_shape=jax.ShapeDtypeStruct((m, n), F32),
        scratch_shapes=[pltpu.VMEM((k, bn), BF16)],
        compiler_params=_params("parallel", "arbitrary"),
        name=name,
    )(a, wt)


def _scan_kernel(*refs, kind, rev, tb, dk, dv):
    if kind == "gla":
        q_ref, k_ref, v_ref, low_ref, u_ref, b_ref = refs[:6]
        rest = refs[6:]
    else:
        q_ref, f_ref, v_ref, lbl_ref = refs[:4]
        rest = refs[4:]
    if rev:
        prev_ref, og_ref, nw_ref, o_ref = rest[:4]
        scr = rest[4:]
    else:
        o_ref = rest[0]
        scr = rest[1:]
    st_ref, q_s, k_s, cum_s, a_s, m_s, o_s = scr

    @pl.when(pl.program_id(1) == 0)
    def _():
        st_ref[...] = jnp.zeros_like(st_ref)

    scale = dk ** -0.5
    if kind == "gla":
        z = jnp.dot(low_ref[...].astype(BF16), u_ref[...],
                    preferred_element_type=F32) + b_ref[...]
        log_sig = jnp.minimum(z, 0.0) - jnp.log1p(jnp.exp(-jnp.abs(z)))
        g = log_sig * (1.0 / GLA_GATE_NORMALIZER)
        q_s[...] = q_ref[...] * scale
        k_s[...] = k_ref[...]
    else:
        logits = lbl_ref[...]
        e = jnp.exp(logits - jnp.max(logits, axis=0, keepdims=True))
        lb = e[0:1] / jnp.sum(e, axis=0, keepdims=True)
        f = f_ref[...]
        t = jnp.exp(-jnp.abs(f))
        s_big = 1.0 / (1.0 + t)
        s_small = t * s_big
        k_s[...] = (1.0 - lb) * jnp.where(f >= 0.0, s_small, s_big)
        g = jnp.log(lb + (1.0 - lb) * jnp.where(f >= 0.0, s_big, s_small))
        qq = q_ref[...]
        q_s[...] = qq * jax.nn.sigmoid(qq) * scale
    worst_bits = jnp.max(-g) * LOG2E

    c = CHUNK
    n_sub = c // SUB
    row_cc = lax.broadcasted_iota(I32, (c, c), 0)
    col_cc = lax.broadcasted_iota(I32, (c, c), 1)
    tri = (col_cc >= row_cc) if rev else (col_cc <= row_cc)
    tri_b = jnp.where(tri, 1.0, 0.0).astype(BF16)
    lane_sc = lax.broadcasted_iota(I32, (SUB, c), 1)
    t_loc = lax.broadcasted_iota(I32, (SUB, 1), 0)

    n_chunk = tb // c
    order = list(range(n_chunk - 1, -1, -1) if rev else range(n_chunk))

    c2s = {}
    for j in order:
        r0 = j * c
        gc = g[r0:r0 + c]
        g_hi = gc.astype(BF16)
        res = gc - g_hi.astype(F32)
        g_mid = res.astype(BF16)
        g_lo = (res - g_mid.astype(F32)).astype(BF16)
        cum = (jnp.dot(tri_b, g_hi, preferred_element_type=F32)
               + jnp.dot(tri_b, g_mid, preferred_element_type=F32)
               + jnp.dot(tri_b, g_lo, preferred_element_type=F32))
        c2 = cum * LOG2E
        cum_s[r0:r0 + c, :] = c2
        c2s[j] = c2

    a_offs = {}
    for j in order:
        r0 = j * c
        c2 = c2s[j]
        qc = q_s[r0:r0 + c]
        kc = k_s[r0:r0 + c]
        a_off = None
        w = c // 2
        while w >= SUB:
            refs = []
            for b in range(0, c, 2 * w):
                row = (b + w) if rev else (b + w - 1)
                refs.append(jnp.broadcast_to(c2[row:row + 1], (2 * w, dk)))
            ref_map = refs[0] if len(refs) == 1 else jnp.concatenate(refs, axis=0)
            fac = jnp.exp2(-jnp.abs(c2 - ref_map))
            zero = jnp.zeros((w, dk), F32)
            q_parts, k_parts = [], []
            for b in range(0, c, 2 * w):
                first, second = slice(b, b + w), slice(b + w, b + 2 * w)
                if rev:
                    q_parts += [qc[first] * fac[first], zero]
                    k_parts += [zero, kc[second] * fac[second]]
                else:
                    q_parts += [zero, qc[second] * fac[second]]
                    k_parts += [kc[first] * fac[first], zero]
            qt = jnp.concatenate(q_parts, axis=0).astype(BF16)
            kt = jnp.concatenate(k_parts, axis=0).astype(BF16)
            a_w = lax.dot_general(qt, kt, NT_DIMS, preferred_element_type=F32)
            if 2 * w < c:
                shift = (2 * w).bit_length() - 1
                same = jnp.right_shift(row_cc, shift) == jnp.right_shift(col_cc, shift)
                a_w = jnp.where(same, a_w, 0.0)
            a_off = a_w if a_off is None else a_off + a_w
            w //= 2
        a_offs[j] = a_off

    for j in order:
        r0 = j * c
        c2 = c2s[j]
        tot2 = c2[0:1] if rev else c2[c - 1:c]
        kdec = (k_s[r0:r0 + c] * jnp.exp2(tot2 - c2)).astype(BF16)
        vb = v_ref[r0:r0 + c].astype(BF16)
        m_s[j] = lax.dot_general(vb, kdec, TN_DIMS, preferred_element_type=F32)

    st = st_ref[...]
    for j in order:
        r0 = j * c
        c2 = c2s[j]
        tot2 = c2[0:1] if rev else c2[c - 1:c]
        qh = (q_s[r0:r0 + c] * jnp.exp2(c2)).astype(BF16)
        o_s[r0:r0 + c, :] = lax.dot_general(qh, st.astype(BF16), NT_DIMS,
                                            preferred_element_type=F32)
        st = jnp.exp2(tot2) * st + m_s[j]
    st_ref[...] = st

    in_sub = jnp.right_shift(row_cc, SUB.bit_length() - 1) == jnp.right_shift(col_cc, SUB.bit_length() - 1)
    for j in order:
        r0 = j * c
        c2 = c2s[j]
        refs = []
        for lo in range(0, c, SUB):
            row = (lo + SUB - 1) if rev else lo
            refs.append(jnp.broadcast_to(c2[row:row + 1], (SUB, dk)))
        d = c2 - jnp.concatenate(refs, axis=0)
        qt = (q_s[r0:r0 + c] * jnp.exp2(d)).astype(BF16)
        kt = (k_s[r0:r0 + c] * jnp.exp2(-d)).astype(BF16)
        a_in = lax.dot_general(qt, kt, NT_DIMS, preferred_element_type=F32)
        a_s[r0:r0 + c, :] = a_offs[j] + jnp.where(in_sub & tri, a_in, 0.0)

    @pl.when(worst_bits > FAST_STEP_BITS)
    def _():
        for j in order:
            r0 = j * c
            c2 = c2s[j]
            qc = q_s[r0:r0 + c]
            blocks = []
            for j1 in range(n_sub):
                lo = j1 * SUB
                qb = qc[lo:lo + SUB]
                cb = c2[lo:lo + SUB]
                blk = jnp.zeros((SUB, c), F32)
                for sl in range(SUB):
                    s = lo + sl
                    cs = cum_s[r0 + s:r0 + s + 1, :]
                    ks = k_s[r0 + s:r0 + s + 1, :]
                    valid_t = (t_loc <= sl) if rev else (t_loc >= sl)
                    p = qb * ks * jnp.exp2(jnp.where(valid_t, cb - cs, -jnp.inf))
                    col = jnp.sum(p, axis=1, keepdims=True)
                    blk = jnp.where(lane_sc == s, col, blk)
                blocks.append(blk)
            a_s[r0:r0 + c, :] = jnp.concatenate(blocks, axis=0) + a_offs[j]

    for j in order:
        r0 = j * c
        vb = v_ref[r0:r0 + c].astype(BF16)
        o_s[r0:r0 + c, :] += jnp.dot(a_s[r0:r0 + c, :].astype(BF16), vb,
                                     preferred_element_type=F32)

    o = o_s[...]
    if rev:
        t = prev_ref[...] + o
        ms = jnp.mean(t * t, axis=-1, keepdims=True)
        y = t * lax.rsqrt(ms + EPS) * nw_ref[...]
        og = og_ref[...]
        o_ref[...] = (y * (og * jax.nn.sigmoid(og))).astype(o_ref.dtype)
    else:
        o_ref[...] = o


def scan(kind, rev, srcs, extras, heads, dk, dv, tb=1024):
    n = srcs["q"][0].shape[0]
    tb = min(tb, n)
    nb = n // tb

    def blk(c):
        return (nb - 1 - c) if rev else c

    def col_spec(width, name):
        base = srcs[name][1] // width
        return pl.BlockSpec((tb, width), lambda h, c: (blk(c), base + h))

    if kind == "gla":
        low, u, bias = extras["low"], extras["u"], extras["bias"]
        operands = [srcs["q"][0], srcs["k"][0], srcs["v"][0], low, u, bias]
        in_specs = [col_spec(dk, "q"), col_spec(dk, "k"), col_spec(dv, "v"),
                    pl.BlockSpec((tb, LANES), lambda h, c: (blk(c), 0)),
                    pl.BlockSpec((LANES, dk), lambda h, c: (0, h)),
                    pl.BlockSpec((1, dk), lambda h, c: (0, h))]
    else:
        lbl = extras["lb_logits"]
        operands = [srcs["q"][0], srcs["f"][0], srcs["v"][0], lbl]
        in_specs = [col_spec(dk, "q"), col_spec(dk, "f"), col_spec(dv, "v"),
                    pl.BlockSpec((lbl.shape[0], dk), lambda h, c: (0, h))]
    if rev:
        operands += [extras["prev"], srcs["og"][0], extras["norm_w"]]
        in_specs += [pl.BlockSpec((tb, dv), lambda h, c: (blk(c), h)),
                     col_spec(dv, "og"),
                     pl.BlockSpec((1, dv), lambda h, c: (0, 0))]
    out_dtype = BF16 if rev else F32
    kern = functools.partial(_scan_kernel, kind=kind, rev=rev, tb=tb, dk=dk, dv=dv)
    return pl.pallas_call(
        kern,
        grid=(heads, nb),
        in_specs=in_specs,
        out_specs=pl.BlockSpec((tb, dv), lambda h, c: (blk(c), h)),
        out_shape=jax.ShapeDtypeStruct((n, heads * dv), out_dtype),
        scratch_shapes=[pltpu.VMEM((dv, dk), F32),
                        pltpu.VMEM((tb, dk), F32),
                        pltpu.VMEM((tb, dk), F32),
                        pltpu.VMEM((tb, dk), F32),
                        pltpu.VMEM((tb, CHUNK), F32),
                        pltpu.VMEM((tb // CHUNK, dv, dk), F32),
                        pltpu.VMEM((tb, dv), F32)],
        compiler_params=_params("parallel", "arbitrary"),
        name=f"scan_{kind}_{'bwd' if rev else 'fwd'}",
    )(*operands)


def _outproj_kernel(a_ref, b_ref, wa_ref, wb_ref, x_ref, o_ref, wa_s, wb_s):
    @pl.when(pl.program_id(1) == 0)
    def _():
        wa_s[...] = wa_ref[...].astype(BF16)
        wb_s[...] = wb_ref[...].astype(BF16)

    acc = jnp.dot(a_ref[...], wa_s[...], preferred_element_type=F32)
    acc += jnp.dot(b_ref[...], wb_s[...], preferred_element_type=F32)
    o_ref[...] = x_ref[...] + acc


def outproj(a, b, w, x, bm=1024, bn=512):
    m, ka = a.shape
    kb = b.shape[1]
    n = w.shape[1]
    assert ka == kb
    bm, bn = min(bm, m), min(bn, n)
    return pl.pallas_call(
        _outproj_kernel,
        grid=(n // bn, m // bm),
        in_specs=[pl.BlockSpec((bm, ka), lambda j, i: (i, 0)),
                  pl.BlockSpec((bm, kb), lambda j, i: (i, 0)),
                  pl.BlockSpec((ka, bn), lambda j, i: (0, j)),
                  pl.BlockSpec((kb, bn), lambda j, i: (1, j)),
                  pl.BlockSpec((bm, bn), lambda j, i: (i, j))],
        out_specs=pl.BlockSpec((bm, bn), lambda j, i: (i, j)),
        out_shape=jax.ShapeDtypeStruct((m, n), F32),
        scratch_shapes=[pltpu.VMEM((ka, bn), BF16), pltpu.VMEM((kb, bn), BF16)],
        compiler_params=_params("parallel", "arbitrary"),
        name="outproj",
    )(a, b, w, w, x)


def _norm_router_kernel(h_ref, w_ref, wr_ref, hn_ref, aff_ref):
    x = h_ref[...]
    ms = jnp.mean(x * x, axis=-1, keepdims=True)
    hn = x * lax.rsqrt(ms + EPS) * w_ref[...]
    d2 = hn.shape[1] // 2
    hn_ref[...] = pltpu.bitcast(
        pltpu.pack_elementwise([hn[:, :d2], hn[:, d2:]], packed_dtype=BF16), U32)
    hn_hi = hn.astype(BF16)
    hn_lo = (hn - hn_hi.astype(F32)).astype(BF16)
    wr = wr_ref[...]
    wr_hi = wr.astype(BF16)
    wr_lo = (wr - wr_hi.astype(F32)).astype(BF16)
    logits = (lax.dot_general(wr_hi, hn_hi, NT_DIMS, preferred_element_type=F32)
              + lax.dot_general(wr_hi, hn_lo, NT_DIMS, preferred_element_type=F32)
              + lax.dot_general(wr_lo, hn_hi, NT_DIMS, preferred_element_type=F32))
    m = jnp.max(logits, axis=0, keepdims=True)
    p = jnp.exp(logits - m)
    aff_ref[...] = p / jnp.sum(p, axis=0, keepdims=True)


def norm_router(h, w, w_router_t, tm=256):
    n, d = h.shape
    e = w_router_t.shape[0]
    return pl.pallas_call(
        _norm_router_kernel,
        grid=(n // tm,),
        in_specs=[pl.BlockSpec((tm, d), lambda i: (i, 0)),
                  pl.BlockSpec((1, d), lambda i: (0, 0)),
                  pl.BlockSpec((e, d), lambda i: (0, 0))],
        out_specs=[pl.BlockSpec((tm, d // 2), lambda i: (i, 0)),
                   pl.BlockSpec((e, tm), lambda i: (0, i))],
        out_shape=[jax.ShapeDtypeStruct((n, d // 2), U32),
                   jax.ShapeDtypeStruct((e, n), F32)],
        compiler_params=_params("parallel"),
        name="norm_router",
    )(h, w.reshape(1, d), w_router_t)


def _prefix_excl(m_f, out_ref, tri_b):
    e, n = m_f.shape
    off = jnp.zeros((e, 1), F32)
    for b in range(n // LANES):
        mb = m_f[:, b * LANES:(b + 1) * LANES]
        inc = jnp.dot(mb.astype(BF16), tri_b, preferred_element_type=F32)
        out_ref[:, b * LANES:(b + 1) * LANES] = inc - mb + off
        off = off + inc[:, LANES - 1:LANES]


def _select_kernel(aff_ref, selrank_ref, cnt_ref, eqr_s, *, cap):
    a = aff_ref[...]
    e, n = a.shape
    bits = jnp.zeros((e, 1), I32)
    for b in range(30, -1, -1):
        cand = bits | (1 << b)
        cnt = jnp.sum(jnp.where(a >= pltpu.bitcast(cand, F32), 1.0, 0.0), axis=1, keepdims=True)
        bits = jnp.where(cnt >= cap, cand, bits)
    thr = pltpu.bitcast(bits, F32)
    gt = a > thr
    eq = a == thr
    n_gt = jnp.sum(jnp.where(gt, 1.0, 0.0), axis=1, keepdims=True)
    need = cap - n_gt
    r = lax.broadcasted_iota(I32, (LANES, LANES), 0)
    cidx = lax.broadcasted_iota(I32, (LANES, LANES), 1)
    tri_b = jnp.where(r <= cidx, 1.0, 0.0).astype(BF16)
    _prefix_excl(jnp.where(eq, 1.0, 0.0), eqr_s, tri_b)
    sel = jnp.where(gt, 1.0, jnp.where(eq, jnp.where(eqr_s[...] < need, 1.0, 0.0), 0.0))
    _prefix_excl(sel, eqr_s, tri_b)
    cnt = eqr_s[...].astype(I32)
    cnt_ref[...] = cnt
    selrank_ref[...] = jnp.where(sel > 0.5, cnt, -1)


def select_topk(aff_t, cap):
    e, n = aff_t.shape
    return pl.pallas_call(
        functools.partial(_select_kernel, cap=cap),
        out_shape=[jax.ShapeDtypeStruct((e, n), I32), jax.ShapeDtypeStruct((e, n), I32)],
        scratch_shapes=[pltpu.VMEM((e, n), F32)],
        compiler_params=pltpu.CompilerParams(vmem_limit_bytes=VMEM_LIMIT),
        name="select_topk",
    )(aff_t)


def _compact_kernel(lo_ref, sr_ref, aff_ref, idx_ref, gate_ref, acc_s, *, cap, tc, rt):
    n = sr_ref.shape[-1]
    n_c = n // tc
    e = pl.program_id(0)
    acc_s[...] = jnp.zeros_like(acc_s)
    rows8 = lax.broadcasted_iota(I32, (8, tc), 0)
    lane8 = lax.broadcasted_iota(I32, (8, tc), 1)
    r_iota = lax.broadcasted_iota(I32, (rt, tc), 0)

    def body(it, carry):
        ci = it // (cap // rt)
        ri = it % (cap // rt)
        c0 = pl.multiple_of(ci * tc, tc)
        q0 = pl.multiple_of(ri * rt, rt)
        lo = lo_ref[e * (n_c + 1) + ci]
        hi = lo_ref[e * (n_c + 1) + ci + 1]

        @pl.when((hi > q0) & (lo < q0 + rt))
        def _():
            sr = sr_ref[0, :, pl.ds(c0, tc)]
            a = aff_ref[0, :, pl.ds(c0, tc)]
            onehot = jnp.where(sr == r_iota + q0, 1.0, 0.0).astype(BF16)
            t = lane8 + c0
            a_hi = a.astype(BF16)
            res = a - a_hi.astype(F32)
            a_mid = res.astype(BF16)
            a_lo = (res - a_mid.astype(F32)).astype(BF16)
            lhs = jnp.where(rows8 == 0, jnp.right_shift(t, 7).astype(F32),
                  jnp.where(rows8 == 1, jnp.bitwise_and(t, LANES - 1).astype(F32),
                  jnp.where(rows8 == 2, a_hi.astype(F32),
                  jnp.where(rows8 == 3, a_mid.astype(F32),
                  jnp.where(rows8 == 4, a_lo.astype(F32), 0.0))))).astype(BF16)
            part = lax.dot_general(lhs, onehot, NT_DIMS, preferred_element_type=F32)
            acc_s[:, pl.ds(q0, rt)] += part

        return carry

    lax.fori_loop(0, n_c * (cap // rt), body, 0)
    acc = acc_s[...]
    idx_ref[0] = (acc[0:1] * float(LANES) + acc[1:2]).astype(I32)
    gate_ref[0] = acc[2:3] + acc[3:4] + acc[4:5]


def compact(selrank, cnt, aff_t, cap, tc=1024, rt=256):
    e, n = selrank.shape
    tc = min(tc, n)
    rt = min(rt, cap)
    sr3 = selrank.reshape(e, 1, n)
    af3 = aff_t.reshape(e, 1, n)
    lo_tab = jnp.concatenate([cnt[:, ::tc], jnp.full((e, 1), cap, I32)], axis=1)
    grid_spec = pltpu.PrefetchScalarGridSpec(
        num_scalar_prefetch=1,
        grid=(e,),
        in_specs=[pl.BlockSpec((1, 1, n), lambda i, lo: (i, 0, 0)),
                  pl.BlockSpec((1, 1, n), lambda i, lo: (i, 0, 0))],
        out_specs=[pl.BlockSpec((1, 1, cap), lambda i, lo: (i, 0, 0)),
                   pl.BlockSpec((1, 1, cap), lambda i, lo: (i, 0, 0))],
        scratch_shapes=[pltpu.VMEM((8, cap), F32)],
    )
    idx, gate = pl.pallas_call(
        functools.partial(_compact_kernel, cap=cap, tc=tc, rt=rt),
        grid_spec=grid_spec,
        out_shape=[jax.ShapeDtypeStruct((e, 1, cap), I32),
                   jax.ShapeDtypeStruct((e, 1, cap), F32)],
        compiler_params=_params("arbitrary"),
        name="compact",
    )(lo_tab.reshape(-1), sr3, af3)
    return idx.reshape(e, cap), gate.reshape(e, cap)


def _ffn_kernel(idx_ref, hp_hbm, gate_ref, w1_ref, w3_ref, w2_ref, y_ref,
                xg_s, xb_s, hid_s, sem, *, cap, nf, bf, n_exp):
    e = pl.program_id(0)
    s = pl.program_id(1)
    d2 = xg_s.shape[1]

    def row_copy(row, r):
        return pltpu.make_async_copy(hp_hbm.at[pl.ds(row, 1), :],
                                     xg_s.at[pl.ds(r, 1), :], sem)

    @pl.when(s == 0)
    def _():
        @pl.when(e == 0)
        def _():
            def issue(r, carry):
                row_copy(idx_ref[r], r).start()
                return carry

            lax.fori_loop(0, cap, issue, 0, unroll=8)

        pltpu.make_async_copy(hp_hbm.at[pl.ds(0, cap), :], xg_s, sem).wait()
        xg = xg_s[...]
        xb_s[:, :d2] = pltpu.unpack_elementwise(
            xg, index=0, packed_dtype=BF16, unpacked_dtype=F32).astype(BF16)
        xb_s[:, d2:] = pltpu.unpack_elementwise(
            xg, index=1, packed_dtype=BF16, unpacked_dtype=F32).astype(BF16)

    @pl.when(s < nf)
    def _():
        nxt = jnp.minimum(e + 1, n_exp - 1)
        share = cap // nf
        for r in range(share):
            slot = s * share + r
            row_copy(idx_ref[nxt * cap + slot], slot).start()

        x = xb_s[...]
        a = jnp.dot(x, w1_ref[0].astype(BF16), preferred_element_type=F32)
        b = jnp.dot(x, w3_ref[0].astype(BF16), preferred_element_type=F32)
        col = pl.multiple_of(s * bf, bf)
        hid_s[:, pl.ds(col, bf)] = (a * jax.nn.sigmoid(a) * b).astype(BF16)

    @pl.when(s >= nf)
    def _():
        y = jnp.dot(hid_s[...], w2_ref[0].astype(BF16), preferred_element_type=F32)
        y_ref[0] = (y * gate_ref[0]).astype(y_ref.dtype)

    @pl.when((e == n_exp - 1) & (s == pl.num_programs(1) - 1))
    def _():
        pltpu.make_async_copy(hp_hbm.at[pl.ds(0, cap), :], xg_s, sem).wait()


def expert_ffn(idx, gate, hn_packed, w1, w3, w2, bf=256, bn=512):
    e, cap = idx.shape
    d2 = hn_packed.shape[1]
    d = 2 * d2
    f = w1.shape[2]
    bf = min(bf, f)
    bn = min(bn, d)
    nf = f // bf
    nn = d // bn
    grid_spec = pltpu.PrefetchScalarGridSpec(
        num_scalar_prefetch=1,
        grid=(e, nf + nn),
        in_specs=[pl.BlockSpec(memory_space=pl.ANY),
                  pl.BlockSpec((1, cap, 1), lambda i, s, idx: (i, 0, 0)),
                  pl.BlockSpec((1, d, bf), lambda i, s, idx: (i, 0, jnp.minimum(s, nf - 1))),
                  pl.BlockSpec((1, d, bf), lambda i, s, idx: (i, 0, jnp.minimum(s, nf - 1))),
                  pl.BlockSpec((1, f, bn), lambda i, s, idx: (i, 0, jnp.maximum(s - nf, 0)))],
        out_specs=pl.BlockSpec((1, cap, bn), lambda i, s, idx: (i, 0, jnp.maximum(s - nf, 0))),
        scratch_shapes=[pltpu.VMEM((cap, d2), U32),
                        pltpu.VMEM((cap, d), BF16),
                        pltpu.VMEM((cap, f), BF16),
                        pltpu.SemaphoreType.DMA(())],
    )
    return pl.pallas_call(
        functools.partial(_ffn_kernel, cap=cap, nf=nf, bf=bf, n_exp=e),
        grid_spec=grid_spec,
        out_shape=jax.ShapeDtypeStruct((e, cap, d), BF16),
        compiler_params=_params("arbitrary", "arbitrary"),
        name="expert_ffn",
    )(idx.reshape(-1), hn_packed, gate.reshape(e, cap, 1), w1, w3, w2)


def _combine_kernel(lo_ref, h_ref, sr_ref, y_hbm, nw_ref, o_ref,
                    acc_s, pack_s, big_s, sem_pack, sem_big,
                    *, n_exp, n_tiles, cap, ws):
    i = pl.program_id(0)
    tt = h_ref.shape[0]
    n_pack = n_exp * ws

    def bounds(tile, e):
        lo = lo_ref[e * (n_tiles + 1) + tile]
        hi = lo_ref[e * (n_tiles + 1) + tile + 1]
        start = jnp.minimum((lo // BF16_SUBLANES) * BF16_SUBLANES, cap - ws)
        return lo, hi, start

    def pack_copy(tile, e, slot):
        _, _, start = bounds(tile, e)
        start = pl.multiple_of(start, BF16_SUBLANES)
        return pltpu.make_async_copy(y_hbm.at[e, pl.ds(start, ws), :],
                                     pack_s.at[slot, pl.ds(e * ws, ws), :],
                                     sem_pack.at[slot])

    def issue(tile, slot):
        for e in range(n_exp):
            pack_copy(tile, e, slot).start()

    @pl.when(i == 0)
    def _():
        issue(0, 0)

    @pl.when(i + 1 < n_tiles)
    def _():
        issue(i + 1, (i + 1) % 2)

    slot = i % 2
    for e in range(n_exp):
        pack_copy(i, e, slot).wait()

    lane = lax.broadcasted_iota(I32, (1, n_pack), 1)
    target = jnp.full((1, n_pack), -2, I32)
    for e in range(n_exp):
        lo, hi, start = bounds(i, e)
        covered = hi <= start + ws
        first = jnp.where(covered, start, -(cap + n_pack))
        in_e = (lane >= e * ws) & (lane < (e + 1) * ws)
        target = jnp.where(in_e, first + lane - e * ws, target)
    per_tile = LANES // ws
    lane_in = lax.broadcasted_iota(I32, (tt, LANES), 1)
    cols = []
    for p in range(n_pack // LANES):
        e0 = p * per_tile
        blk = jnp.broadcast_to(sr_ref[:, e0:e0 + 1], (tt, LANES))
        for q in range(1, per_tile):
            blk = jnp.where(lane_in >= q * ws, sr_ref[:, e0 + q:e0 + q + 1], blk)
        cols.append(blk)
    sr_wide = jnp.concatenate(cols, axis=1)
    hit = jnp.where(sr_wide == target, 1.0, 0.0).astype(BF16)
    acc_s[...] = h_ref[...] + jnp.dot(hit, pack_s[slot], preferred_element_type=F32)

    wl = big_s.shape[0]
    j_big = lax.broadcasted_iota(I32, (tt, wl), 1)
    for e in range(n_exp):
        lo, hi, start = bounds(i, e)
        start_l = jnp.minimum(start, cap - wl)
        sr = sr_ref[:, e:e + 1]

        def window(begin, first_slot, e=e, sr=sr):
            begin = pl.multiple_of(begin, BF16_SUBLANES)
            cp = pltpu.make_async_copy(y_hbm.at[e, pl.ds(begin, wl), :], big_s, sem_big)
            cp.start()
            cp.wait()
            slot_id = j_big + begin
            hit_l = jnp.where(sr == slot_id, jnp.where(slot_id >= first_slot, 1.0, 0.0), 0.0)
            acc_s[...] += jnp.dot(hit_l.astype(BF16), big_s[...], preferred_element_type=F32)

        @pl.when(hi > start + ws)
        def _():
            window(start_l, lo)

        @pl.when(hi > start_l + wl)
        def _():
            window(jnp.minimum(start_l + wl, cap - wl), start_l + wl)

    x = acc_s[...]
    ms = jnp.mean(x * x, axis=-1, keepdims=True)
    o_ref[...] = x * lax.rsqrt(ms + EPS) * nw_ref[...]


def combine(h, selrank_t, lo_tab, y, norm_w, tt=256, ws=64):
    n, d = h.shape
    e, cap, _ = y.shape
    tt = min(tt, cap, n)
    ws = min(ws, tt)
    assert LANES % ws == 0
    n_tiles = n // tt
    grid_spec = pltpu.PrefetchScalarGridSpec(
        num_scalar_prefetch=1,
        grid=(n_tiles,),
        in_specs=[pl.BlockSpec((tt, d), lambda i, lo: (i, 0)),
                  pl.BlockSpec((tt, e), lambda i, lo: (i, 0)),
                  pl.BlockSpec(memory_space=pl.ANY),
                  pl.BlockSpec((1, d), lambda i, lo: (0, 0))],
        out_specs=pl.BlockSpec((tt, d), lambda i, lo: (i, 0)),
        scratch_shapes=[pltpu.VMEM((tt, d), F32),
                        pltpu.VMEM((2, e * ws, d), BF16),
                        pltpu.VMEM((tt, d), BF16),
                        pltpu.SemaphoreType.DMA((2,)),
                        pltpu.SemaphoreType.DMA(())],
    )
    return pl.pallas_call(
        functools.partial(_combine_kernel, n_exp=e, n_tiles=n_tiles, cap=cap, ws=ws),
        grid_spec=grid_spec,
        out_shape=jax.ShapeDtypeStruct((n, d), F32),
        compiler_params=_params("arbitrary"),
        name="combine",
    )(lo_tab.reshape(-1), h, selrank_t, y, norm_w.reshape(1, d))


def moe_block(h, norm_w, w_router, w1, w3, w2, norm_final_w):
    n, _ = h.shape
    e = w_router.shape[1]
    cap = CAPACITY_FACTOR * n // e
    hn_packed, aff_t = norm_router(h, norm_w, w_router.T)
    selrank, cnt = select_topk(aff_t, cap)
    idx, gate = compact(selrank, cnt, aff_t, cap)
    y = expert_ffn(idx, gate, hn_packed, w1, w3, w2)
    tt = min(256, cap, n)
    lo_tab = jnp.concatenate([cnt[:, ::tt], jnp.full((e, 1), cap, I32)], axis=1)
    return combine(h, selrank.T, lo_tab, y, norm_final_w, tt=tt)


def mixer_block(x, norm_w, w_in, up_f, bias_f, up_b, bias_b, gla_norm_w,
                lb_logits_f, lb_logits_b, hgrn_norm_w, w_out):
    gk = GLA_HEADS * GLA_DK
    gv = GLA_HEADS * GLA_DV
    hd = HGRN_HEADS * HGRN_D
    r = GLA_GATE_RANK
    n_a = 2 * gk + gv
    o_b = n_a + 2 * r
    n_b = w_in.shape[1] - o_b
    wt = jnp.swapaxes(w_in, 0, 1)

    xn = rmsnorm(x, norm_w, BF16)
    proj_a = matmul_nt(xn, wt, 0, n_a, 1024, 512, "inproj_a")
    proj_b = matmul_nt(xn, wt, o_b, n_b, 1024, 512, "inproj_b")
    low = matmul_nt(xn, wt, n_a, LANES, 1024, LANES, "inproj_gate")

    src_gla = {"q": (proj_a, 0), "k": (proj_a, gk), "v": (proj_a, 2 * gk), "og": (proj_b, 0)}
    src_hg = {"q": (proj_b, gv), "v": (proj_b, gv + 3 * hd), "og": (proj_b, gv + 4 * hd)}
    u_f = jnp.zeros((LANES, gk), F32).at[0:r].set(up_f).astype(BF16)
    u_b = jnp.zeros((LANES, gk), F32).at[r:2 * r].set(up_b).astype(BF16)
    gla_f = scan("gla", False, src_gla, {"low": low, "u": u_f, "bias": bias_f.reshape(1, gk)},
                 GLA_HEADS, GLA_DK, GLA_DV)
    gla_o = scan("gla", True, src_gla,
                 {"low": low, "u": u_b, "bias": bias_b.reshape(1, gk), "prev": gla_f,
                  "norm_w": gla_norm_w.reshape(1, GLA_DV)},
                 GLA_HEADS, GLA_DK, GLA_DV)
    hg_f = scan("hgrn", False, dict(src_hg, f=(proj_b, gv + hd)), {"lb_logits": lb_logits_f},
                HGRN_HEADS, HGRN_D, HGRN_D)
    hg_o = scan("hgrn", True, dict(src_hg, f=(proj_b, gv + 2 * hd)),
                {"lb_logits": lb_logits_b, "prev": hg_f,
                 "norm_w": hgrn_norm_w.reshape(1, HGRN_D)},
                HGRN_HEADS, HGRN_D, HGRN_D)
    return outproj(gla_o, hg_o, w_out, x)


def kernel(x, norm_mix_w, w_in, gla_gate_up_f, gla_gate_bias_f, gla_gate_up_b, gla_gate_bias_b, gla_norm_w, hgrn_lb_logits_f, hgrn_lb_logits_b, hgrn_norm_w, w_out, norm_ffn_w, w_router, expert_w1, expert_w3, expert_w2, norm_final_w):
    b, l, d = x.shape
    outs = []
    for bi in range(b):
        xb = x.reshape(l, d) if b == 1 else x[bi]
        h = mixer_block(xb, norm_mix_w[0], w_in[0], gla_gate_up_f[0], gla_gate_bias_f[0],
                        gla_gate_up_b[0], gla_gate_bias_b[0], gla_norm_w[0],
                        hgrn_lb_logits_f, hgrn_lb_logits_b, hgrn_norm_w[0], w_out[0])
        outs.append(moe_block(h, norm_ffn_w[0], w_router[0], expert_w1[0], expert_w3[0],
                              expert_w2[0], norm_final_w))
    return outs[0].reshape(1, l, d) if b == 1 else jnp.stack(outs, axis=0)
```

```python
import functools

import jax
import jax.numpy as jnp
from jax import lax
from jax.experimental import pallas as pl
from jax.experimental.pallas import tpu as pltpu

F32 = jnp.float32
BF16 = jnp.bfloat16
I32 = jnp.int32
U32 = jnp.uint32

EPS = 1e-6
LANES = 128
F32_SUBLANES = 8
BF16_SUBLANES = 16
VMEM_LIMIT = 56 * 1024 * 1024

GLA_HEADS = 8
GLA_DK = 128
GLA_DV = 256
GLA_GATE_RANK = 16
GLA_GATE_NORMALIZER = 16.0
HGRN_HEADS = 16
HGRN_D = 128
CHUNK = 64
SUB = 8
LOG2E = 1.4426950408889634
FAST_STEP_BITS = 12.0
N_EXPERTS = 16
CAPACITY_FACTOR = 2

NT_DIMS = (((1,), (1,)), ((), ()))
TN_DIMS = (((0,), (0,)), ((), ()))


def _params(*sem):
    return pltpu.CompilerParams(dimension_semantics=sem, vmem_limit_bytes=VMEM_LIMIT)


def _rmsnorm_kernel(x_ref, w_ref, o_ref):
    x = x_ref[...]
    ms = jnp.mean(x * x, axis=-1, keepdims=True)
    o_ref[...] = (x * lax.rsqrt(ms + EPS) * w_ref[...]).astype(o_ref.dtype)


def rmsnorm(x, w, out_dtype, tm=256):
    n, d = x.shape
    return pl.pallas_call(
        _rmsnorm_kernel,
        grid=(n // tm,),
        in_specs=[pl.BlockSpec((tm, d), lambda i: (i, 0)),
                  pl.BlockSpec((1, d), lambda i: (0, 0))],
        out_specs=pl.BlockSpec((tm, d), lambda i: (i, 0)),
        out_shape=jax.ShapeDtypeStruct((n, d), out_dtype),
        compiler_params=_params("parallel"),
        name="rmsnorm",
    )(x, w.reshape(1, d))


def _mm_nt_kernel(a_ref, w_ref, o_ref):
    o_ref[...] = lax.dot_general(a_ref[...], w_ref[...].astype(BF16), NT_DIMS,
                                 preferred_element_type=F32)


def matmul_nt(a, wt, row0, n, bm, bn, name):
    m, k = a.shape
    bm, bn = min(bm, m), min(bn, n)
    if row0 % bn == 0:
        w_spec = pl.BlockSpec((bn, k), lambda i, j: (row0 // bn + j, 0))
    else:
        assert row0 % F32_SUBLANES == 0 and bn % F32_SUBLANES == 0
        w_spec = pl.BlockSpec(
            (pl.Element(bn), pl.Element(k)),
            lambda i, j: ((row0 // F32_SUBLANES + j * (bn // F32_SUBLANES)) * F32_SUBLANES, 0))
    return pl.pallas_call(
        _mm_nt_kernel,
        grid=(m // bm, n // bn),
        in_specs=[pl.BlockSpec((bm, k), lambda i, j: (i, 0)), w_spec],
        out_specs=pl.BlockSpec((bm, bn), lambda i, j: (i, j)),
        out_shape=jax.ShapeDtypeStruct((m, n), F32),
        compiler_params=_params("parallel", "parallel"),
        name=name,
    )(a, wt)


def _scan_kernel(*refs, kind, rev, tb, dk, dv):
    if kind == "gla":
        q_ref, k_ref, v_ref, low_ref, u_ref, b_ref = refs[:6]
        rest = refs[6:]
    else:
        q_ref, f_ref, v_ref, lbl_ref = refs[:4]
        rest = refs[4:]
    if rev:
        prev_ref, og_ref, nw_ref, o_ref = rest[:4]
        scr = rest[4:]
    else:
        o_ref = rest[0]
        scr = rest[1:]
    st_ref, q_s, k_s, cum_s, a_s, m_s, o_s = scr

    @pl.when(pl.program_id(1) == 0)
    def _():
        st_ref[...] = jnp.zeros_like(st_ref)

    scale = dk ** -0.5
    if kind == "gla":
        z = jnp.dot(low_ref[...].astype(BF16), u_ref[...],
                    preferred_element_type=F32) + b_ref[...]
        log_sig = jnp.minimum(z, 0.0) - jnp.log1p(jnp.exp(-jnp.abs(z)))
        g = log_sig * (1.0 / GLA_GATE_NORMALIZER)
        q_s[...] = q_ref[...] * scale
        k_s[...] = k_ref[...]
    else:
        logits = lbl_ref[...]
        e = jnp.exp(logits - jnp.max(logits, axis=0, keepdims=True))
        lb = e[0:1] / jnp.sum(e, axis=0, keepdims=True)
        f = f_ref[...]
        t = jnp.exp(-jnp.abs(f))
        s_big = 1.0 / (1.0 + t)
        s_small = t * s_big
        k_s[...] = (1.0 - lb) * jnp.where(f >= 0.0, s_small, s_big)
        g = jnp.log(lb + (1.0 - lb) * jnp.where(f >= 0.0, s_big, s_small))
        qq = q_ref[...]
        q_s[...] = qq * jax.nn.sigmoid(qq) * scale
    worst_bits = jnp.max(-g) * LOG2E

    c = CHUNK
    n_sub = c // SUB
    row_cc = lax.broadcasted_iota(I32, (c, c), 0)
    col_cc = lax.broadcasted_iota(I32, (c, c), 1)
    tri = (col_cc >= row_cc) if rev else (col_cc <= row_cc)
    tri_b = jnp.where(tri, 1.0, 0.0).astype(BF16)
    lane_sc = lax.broadcasted_iota(I32, (SUB, c), 1)
    t_loc = lax.broadcasted_iota(I32, (SUB, 1), 0)

    n_chunk = tb // c
    order = list(range(n_chunk - 1, -1, -1) if rev else range(n_chunk))

    c2s = {}
    for j in order:
        r0 = j * c
        gc = g[r0:r0 + c]
        g_hi = gc.astype(BF16)
        res = gc - g_hi.astype(F32)
        g_mid = res.astype(BF16)
        g_lo = (res - g_mid.astype(F32)).astype(BF16)
        cum = (jnp.dot(tri_b, g_hi, preferred_element_type=F32)
               + jnp.dot(tri_b, g_mid, preferred_element_type=F32)
               + jnp.dot(tri_b, g_lo, preferred_element_type=F32))
        c2 = cum * LOG2E
        cum_s[r0:r0 + c, :] = c2
        c2s[j] = c2

    a_offs = {}
    for j in order:
        r0 = j * c
        c2 = c2s[j]
        qc = q_s[r0:r0 + c]
        kc = k_s[r0:r0 + c]
        a_off = None
        w = c // 2
        while w >= SUB:
            refs = []
            for b in range(0, c, 2 * w):
                row = (b + w) if rev else (b + w - 1)
                refs.append(jnp.broadcast_to(c2[row:row + 1], (2 * w, dk)))
            ref_map = refs[0] if len(refs) == 1 else jnp.concatenate(refs, axis=0)
            fac = jnp.exp2(-jnp.abs(c2 - ref_map))
            zero = jnp.zeros((w, dk), F32)
            q_parts, k_parts = [], []
            for b in range(0, c, 2 * w):
                first, second = slice(b, b + w), slice(b + w, b + 2 * w)
                if rev:
                    q_parts += [qc[first] * fac[first], zero]
                    k_parts += [zero, kc[second] * fac[second]]
                else:
                    q_parts += [zero, qc[second] * fac[second]]
                    k_parts += [kc[first] * fac[first], zero]
            qt = jnp.concatenate(q_parts, axis=0).astype(BF16)
            kt = jnp.concatenate(k_parts, axis=0).astype(BF16)
            a_w = lax.dot_general(qt, kt, NT_DIMS, preferred_element_type=F32)
            if 2 * w < c:
                shift = (2 * w).bit_length() - 1
                same = jnp.right_shift(row_cc, shift) == jnp.right_shift(col_cc, shift)
                a_w = jnp.where(same, a_w, 0.0)
            a_off = a_w if a_off is None else a_off + a_w
            w //= 2
        a_offs[j] = a_off

    for j in order:
        r0 = j * c
        c2 = c2s[j]
        tot2 = c2[0:1] if rev else c2[c - 1:c]
        kdec = (k_s[r0:r0 + c] * jnp.exp2(tot2 - c2)).astype(BF16)
        vb = v_ref[r0:r0 + c].astype(BF16)
        m_s[j] = lax.dot_general(vb, kdec, TN_DIMS, preferred_element_type=F32)

    st = st_ref[...]
    for j in order:
        r0 = j * c
        c2 = c2s[j]
        tot2 = c2[0:1] if rev else c2[c - 1:c]
        qh = (q_s[r0:r0 + c] * jnp.exp2(c2)).astype(BF16)
        o_s[r0:r0 + c, :] = lax.dot_general(qh, st.astype(BF16), NT_DIMS,
                                            preferred_element_type=F32)
        st = jnp.exp2(tot2) * st + m_s[j]
    st_ref[...] = st

    in_sub = jnp.right_shift(row_cc, SUB.bit_length() - 1) == jnp.right_shift(col_cc, SUB.bit_length() - 1)
    for j in order:
        r0 = j * c
        c2 = c2s[j]
        refs = []
        for lo in range(0, c, SUB):
            row = (lo + SUB - 1) if rev else lo
            refs.append(jnp.broadcast_to(c2[row:row + 1], (SUB, dk)))
        d = c2 - jnp.concatenate(refs, axis=0)
        qt = (q_s[r0:r0 + c] * jnp.exp2(d)).astype(BF16)
        kt = (k_s[r0:r0 + c] * jnp.exp2(-d)).astype(BF16)
        a_in = lax.dot_general(qt, kt, NT_DIMS, preferred_element_type=F32)
        a_s[r0:r0 + c, :] = a_offs[j] + jnp.where(in_sub & tri, a_in, 0.0)

    @pl.when(worst_bits > FAST_STEP_BITS)
    def _():
        for j in order:
            r0 = j * c
            c2 = c2s[j]
            qc = q_s[r0:r0 + c]
            blocks = []
            for j1 in range(n_sub):
                lo = j1 * SUB
                qb = qc[lo:lo + SUB]
                cb = c2[lo:lo + SUB]
                blk = jnp.zeros((SUB, c), F32)
                for sl in range(SUB):
                    s = lo + sl
                    cs = cum_s[r0 + s:r0 + s + 1, :]
                    ks = k_s[r0 + s:r0 + s + 1, :]
                    valid_t = (t_loc <= sl) if rev else (t_loc >= sl)
                    p = qb * ks * jnp.exp2(jnp.where(valid_t, cb - cs, -jnp.inf))
                    col = jnp.sum(p, axis=1, keepdims=True)
                    blk = jnp.where(lane_sc == s, col, blk)
                blocks.append(blk)
            a_s[r0:r0 + c, :] = jnp.concatenate(blocks, axis=0) + a_offs[j]

    for j in order:
        r0 = j * c
        vb = v_ref[r0:r0 + c].astype(BF16)
        o_s[r0:r0 + c, :] += jnp.dot(a_s[r0:r0 + c, :].astype(BF16), vb,
                                     preferred_element_type=F32)

    o = o_s[...]
    if rev:
        t = prev_ref[...] + o
        ms = jnp.mean(t * t, axis=-1, keepdims=True)
        y = t * lax.rsqrt(ms + EPS) * nw_ref[...]
        og = og_ref[...]
        o_ref[...] = (y * (og * jax.nn.sigmoid(og))).astype(o_ref.dtype)
    else:
        o_ref[...] = o


def scan(kind, rev, srcs, extras, heads, dk, dv, tb=1024):
    n = srcs["q"][0].shape[0]
    tb = min(tb, n)
    nb = n // tb

    def blk(c):
        return (nb - 1 - c) if rev else c

    def col_spec(width, name):
        base = srcs[name][1] // width
        return pl.BlockSpec((tb, width), lambda h, c: (blk(c), base + h))

    if kind == "gla":
        low, u, bias = extras["low"], extras["u"], extras["bias"]
        operands = [srcs["q"][0], srcs["k"][0], srcs["v"][0], low, u, bias]
        in_specs = [col_spec(dk, "q"), col_spec(dk, "k"), col_spec(dv, "v"),
                    pl.BlockSpec((tb, LANES), lambda h, c: (blk(c), 0)),
                    pl.BlockSpec((LANES, dk), lambda h, c: (0, h)),
                    pl.BlockSpec((1, dk), lambda h, c: (0, h))]
    else:
        lbl = extras["lb_logits"]
        operands = [srcs["q"][0], srcs["f"][0], srcs["v"][0], lbl]
        in_specs = [col_spec(dk, "q"), col_spec(dk, "f"), col_spec(dv, "v"),
                    pl.BlockSpec((lbl.shape[0], dk), lambda h, c: (0, h))]
    if rev:
        operands += [extras["prev"], srcs["og"][0], extras["norm_w"]]
        in_specs += [pl.BlockSpec((tb, dv), lambda h, c: (blk(c), h)),
                     col_spec(dv, "og"),
                     pl.BlockSpec((1, dv), lambda h, c: (0, 0))]
    out_dtype = BF16 if rev else F32
    kern = functools.partial(_scan_kernel, kind=kind, rev=rev, tb=tb, dk=dk, dv=dv)
    return pl.pallas_call(
        kern,
        grid=(heads, nb),
        in_specs=in_specs,
        out_specs=pl.BlockSpec((tb, dv), lambda h, c: (blk(c), h)),
        out_shape=jax.ShapeDtypeStruct((n, heads * dv), out_dtype),
        scratch_shapes=[pltpu.VMEM((dv, dk), F32),
                        pltpu.VMEM((tb, dk), F32),
                        pltpu.VMEM((tb, dk), F32),
                        pltpu.VMEM((tb, dk), F32),
                        pltpu.VMEM((tb, CHUNK), F32),
                        pltpu.VMEM((tb // CHUNK, dv, dk), F32),
                        pltpu.VMEM((tb, dv), F32)],
        compiler_params=_params("parallel", "arbitrary"),
        name=f"scan_{kind}_{'bwd' if rev else 'fwd'}",
    )(*operands)


def _outproj_kernel(a_ref, b_ref, wa_ref, wb_ref, x_ref, o_ref, wa_s, wb_s):
    @pl.when(pl.program_id(1) == 0)
    def _():
        wa_s[...] = wa_ref[...].astype(BF16)
        wb_s[...] = wb_ref[...].astype(BF16)

    acc = jnp.dot(a_ref[...], wa_s[...], preferred_element_type=F32)
    acc += jnp.dot(b_ref[...], wb_s[...], preferred_element_type=F32)
    o_ref[...] = x_ref[...] + acc


def outproj(a, b, w, x, bm=1024, bn=512):
    m, ka = a.shape
    kb = b.shape[1]
    n = w.shape[1]
    assert ka == kb
    bm, bn = min(bm, m), min(bn, n)
    return pl.pallas_call(
        _outproj_kernel,
        grid=(n // bn, m // bm),
        in_specs=[pl.BlockSpec((bm, ka), lambda j, i: (i, 0)),
                  pl.BlockSpec((bm, kb), lambda j, i: (i, 0)),
                  pl.BlockSpec((ka, bn), lambda j, i: (0, j)),
                  pl.BlockSpec((kb, bn), lambda j, i: (1, j)),
                  pl.BlockSpec((bm, bn), lambda j, i: (i, j))],
        out_specs=pl.BlockSpec((bm, bn), lambda j, i: (i, j)),
        out_shape=jax.ShapeDtypeStruct((m, n), F32),
        scratch_shapes=[pltpu.VMEM((ka, bn), BF16), pltpu.VMEM((kb, bn), BF16)],
        compiler_params=_params("parallel", "arbitrary"),
        name="outproj",
    )(a, b, w, w, x)


def _norm_router_kernel(h_ref, w_ref, wr_ref, hn_ref, aff_ref):
    x = h_ref[...]
    ms = jnp.mean(x * x, axis=-1, keepdims=True)
    hn = x * lax.rsqrt(ms + EPS) * w_ref[...]
    d2 = hn.shape[1] // 2
    hn_ref[...] = pltpu.bitcast(
        pltpu.pack_elementwise([hn[:, :d2], hn[:, d2:]], packed_dtype=BF16), U32)
    hn_hi = hn.astype(BF16)
    hn_lo = (hn - hn_hi.astype(F32)).astype(BF16)
    wr = wr_ref[...]
    wr_hi = wr.astype(BF16)
    wr_lo = (wr - wr_hi.astype(F32)).astype(BF16)
    logits = (lax.dot_general(wr_hi, hn_hi, NT_DIMS, preferred_element_type=F32)
              + lax.dot_general(wr_hi, hn_lo, NT_DIMS, preferred_element_type=F32)
              + lax.dot_general(wr_lo, hn_hi, NT_DIMS, preferred_element_type=F32))
    m = jnp.max(logits, axis=0, keepdims=True)
    p = jnp.exp(logits - m)
    aff_ref[...] = p / jnp.sum(p, axis=0, keepdims=True)


def norm_router(h, w, w_router_t, tm=256):
    n, d = h.shape
    e = w_router_t.shape[0]
    return pl.pallas_call(
        _norm_router_kernel,
        grid=(n // tm,),
        in_specs=[pl.BlockSpec((tm, d), lambda i: (i, 0)),
                  pl.BlockSpec((1, d), lambda i: (0, 0)),
                  pl.BlockSpec((e, d), lambda i: (0, 0))],
        out_specs=[pl.BlockSpec((tm, d // 2), lambda i: (i, 0)),
                   pl.BlockSpec((e, tm), lambda i: (0, i))],
        out_shape=[jax.ShapeDtypeStruct((n, d // 2), U32),
                   jax.ShapeDtypeStruct((e, n), F32)],
        compiler_params=_params("parallel"),
        name="norm_router",
    )(h, w.reshape(1, d), w_router_t)


def _prefix_excl(m_f, out_ref, tri_b):
    e, n = m_f.shape
    off = jnp.zeros((e, 1), F32)
    for b in range(n // LANES):
        mb = m_f[:, b * LANES:(b + 1) * LANES]
        inc = jnp.dot(mb.astype(BF16), tri_b, preferred_element_type=F32)
        out_ref[:, b * LANES:(b + 1) * LANES] = inc - mb + off
        off = off + inc[:, LANES - 1:LANES]


def _select_kernel(aff_ref, selrank_ref, cnt_ref, eqr_s, *, cap):
    a = aff_ref[...]
    e, n = a.shape
    bits = jnp.zeros((e, 1), I32)
    for b in range(30, -1, -1):
        cand = bits | (1 << b)
        cnt = jnp.sum(jnp.where(a >= pltpu.bitcast(cand, F32), 1.0, 0.0), axis=1, keepdims=True)
        bits = jnp.where(cnt >= cap, cand, bits)
    thr = pltpu.bitcast(bits, F32)
    gt = a > thr
    eq = a == thr
    n_gt = jnp.sum(jnp.where(gt, 1.0, 0.0), axis=1, keepdims=True)
    need = cap - n_gt
    r = lax.broadcasted_iota(I32, (LANES, LANES), 0)
    cidx = lax.broadcasted_iota(I32, (LANES, LANES), 1)
    tri_b = jnp.where(r <= cidx, 1.0, 0.0).astype(BF16)
    _prefix_excl(jnp.where(eq, 1.0, 0.0), eqr_s, tri_b)
    sel = jnp.where(gt, 1.0, jnp.where(eq, jnp.where(eqr_s[...] < need, 1.0, 0.0), 0.0))
    _prefix_excl(sel, eqr_s, tri_b)
    cnt = eqr_s[...].astype(I32)
    cnt_ref[...] = cnt
    selrank_ref[...] = jnp.where(sel > 0.5, cnt, -1)


def select_topk(aff_t, cap):
    e, n = aff_t.shape
    return pl.pallas_call(
        functools.partial(_select_kernel, cap=cap),
        out_shape=[jax.ShapeDtypeStruct((e, n), I32), jax.ShapeDtypeStruct((e, n), I32)],
        scratch_shapes=[pltpu.VMEM((e, n), F32)],
        compiler_params=pltpu.CompilerParams(vmem_limit_bytes=VMEM_LIMIT),
        name="select_topk",
    )(aff_t)


def _compact_kernel(lo_ref, sr_ref, aff_ref, idx_ref, gate_ref, acc_s, *, cap, tc, rt):
    n = sr_ref.shape[-1]
    n_c = n // tc
    e = pl.program_id(0)
    acc_s[...] = jnp.zeros_like(acc_s)
    rows8 = lax.broadcasted_iota(I32, (8, tc), 0)
    lane8 = lax.broadcasted_iota(I32, (8, tc), 1)
    r_iota = lax.broadcasted_iota(I32, (rt, tc), 0)

    def body(it, carry):
        ci = it // (cap // rt)
        ri = it % (cap // rt)
        c0 = pl.multiple_of(ci * tc, tc)
        q0 = pl.multiple_of(ri * rt, rt)
        lo = lo_ref[e * (n_c + 1) + ci]
        hi = lo_ref[e * (n_c + 1) + ci + 1]

        @pl.when((hi > q0) & (lo < q0 + rt))
        def _():
            sr = sr_ref[0, :, pl.ds(c0, tc)]
            a = aff_ref[0, :, pl.ds(c0, tc)]
            onehot = jnp.where(sr == r_iota + q0, 1.0, 0.0).astype(BF16)
            t = lane8 + c0
            a_hi = a.astype(BF16)
            res = a - a_hi.astype(F32)
            a_mid = res.astype(BF16)
            a_lo = (res - a_mid.astype(F32)).astype(BF16)
            lhs = jnp.where(rows8 == 0, jnp.right_shift(t, 7).astype(F32),
                  jnp.where(rows8 == 1, jnp.bitwise_and(t, LANES - 1).astype(F32),
                  jnp.where(rows8 == 2, a_hi.astype(F32),
                  jnp.where(rows8 == 3, a_mid.astype(F32),
                  jnp.where(rows8 == 4, a_lo.astype(F32), 0.0))))).astype(BF16)
            part = lax.dot_general(lhs, onehot, NT_DIMS, preferred_element_type=F32)
            acc_s[:, pl.ds(q0, rt)] += part

        return carry

    lax.fori_loop(0, n_c * (cap // rt), body, 0)
    acc = acc_s[...]
    idx_ref[0] = (acc[0:1] * float(LANES) + acc[1:2]).astype(I32)
    gate_ref[0] = acc[2:3] + acc[3:4] + acc[4:5]


def compact(selrank, cnt, aff_t, cap, tc=1024, rt=256):
    e, n = selrank.shape
    tc = min(tc, n)
    rt = min(rt, cap)
    sr3 = selrank.reshape(e, 1, n)
    af3 = aff_t.reshape(e, 1, n)
    lo_tab = jnp.concatenate([cnt[:, ::tc], jnp.full((e, 1), cap, I32)], axis=1)
    grid_spec = pltpu.PrefetchScalarGridSpec(
        num_scalar_prefetch=1,
        grid=(e,),
        in_specs=[pl.BlockSpec((1, 1, n), lambda i, lo: (i, 0, 0)),
                  pl.BlockSpec((1, 1, n), lambda i, lo: (i, 0, 0))],
        out_specs=[pl.BlockSpec((1, 1, cap), lambda i, lo: (i, 0, 0)),
                   pl.BlockSpec((1, 1, cap), lambda i, lo: (i, 0, 0))],
        scratch_shapes=[pltpu.VMEM((8, cap), F32)],
    )
    idx, gate = pl.pallas_call(
        functools.partial(_compact_kernel, cap=cap, tc=tc, rt=rt),
        grid_spec=grid_spec,
        out_shape=[jax.ShapeDtypeStruct((e, 1, cap), I32),
                   jax.ShapeDtypeStruct((e, 1, cap), F32)],
        compiler_params=_params("arbitrary"),
        name="compact",
    )(lo_tab.reshape(-1), sr3, af3)
    return idx.reshape(e, cap), gate.reshape(e, cap)


def _ffn_kernel(idx_ref, hp_hbm, gate_ref, w1_ref, w3_ref, w2_ref, y_ref,
                xg_s, xb_s, hid_s, sem, *, cap, nf, bf, n_exp):
    e = pl.program_id(0)
    s = pl.program_id(1)
    d2 = xg_s.shape[1]

    def row_copy(row, r):
        return pltpu.make_async_copy(hp_hbm.at[pl.ds(row, 1), :],
                                     xg_s.at[pl.ds(r, 1), :], sem)

    @pl.when(s == 0)
    def _():
        @pl.when(e == 0)
        def _():
            def issue(r, carry):
                row_copy(idx_ref[r], r).start()
                return carry

            lax.fori_loop(0, cap, issue, 0, unroll=8)

        pltpu.make_async_copy(hp_hbm.at[pl.ds(0, cap), :], xg_s, sem).wait()
        xg = xg_s[...]
        xb_s[:, :d2] = pltpu.unpack_elementwise(
            xg, index=0, packed_dtype=BF16, unpacked_dtype=F32).astype(BF16)
        xb_s[:, d2:] = pltpu.unpack_elementwise(
            xg, index=1, packed_dtype=BF16, unpacked_dtype=F32).astype(BF16)

    @pl.when(s < nf)
    def _():
        nxt = jnp.minimum(e + 1, n_exp - 1)
        share = cap // nf
        for r in range(share):
            slot = s * share + r
            row_copy(idx_ref[nxt * cap + slot], slot).start()

        x = xb_s[...]
        a = jnp.dot(x, w1_ref[0].astype(BF16), preferred_element_type=F32)
        b = jnp.dot(x, w3_ref[0].astype(BF16), preferred_element_type=F32)
        col = pl.multiple_of(s * bf, bf)
        hid_s[:, pl.ds(col, bf)] = (a * jax.nn.sigmoid(a) * b).astype(BF16)

    @pl.when(s >= nf)
    def _():
        y = jnp.dot(hid_s[...], w2_ref[0].astype(BF16), preferred_element_type=F32)
        y_ref[0] = (y * gate_ref[0]).astype(y_ref.dtype)

    @pl.when((e == n_exp - 1) & (s == pl.num_programs(1) - 1))
    def _():
        pltpu.make_async_copy(hp_hbm.at[pl.ds(0, cap), :], xg_s, sem).wait()


def expert_ffn(idx, gate, hn_packed, w1, w3, w2, bf=256, bn=512):
    e, cap = idx.shape
    d2 = hn_packed.shape[1]
    d = 2 * d2
    f = w1.shape[2]
    bf = min(bf, f)
    bn = min(bn, d)
    nf = f // bf
    nn = d // bn
    grid_spec = pltpu.PrefetchScalarGridSpec(
        num_scalar_prefetch=1,
        grid=(e, nf + nn),
        in_specs=[pl.BlockSpec(memory_space=pl.ANY),
                  pl.BlockSpec((1, cap, 1), lambda i, s, idx: (i, 0, 0)),
                  pl.BlockSpec((1, d, bf), lambda i, s, idx: (i, 0, jnp.minimum(s, nf - 1))),
                  pl.BlockSpec((1, d, bf), lambda i, s, idx: (i, 0, jnp.minimum(s, nf - 1))),
                  pl.BlockSpec((1, f, bn), lambda i, s, idx: (i, 0, jnp.maximum(s - nf, 0)))],
        out_specs=pl.BlockSpec((1, cap, bn), lambda i, s, idx: (i, 0, jnp.maximum(s - nf, 0))),
        scratch_shapes=[pltpu.VMEM((cap, d2), U32),
                        pltpu.VMEM((cap, d), BF16),
                        pltpu.VMEM((cap, f), BF16),
                        pltpu.SemaphoreType.DMA(())],
    )
    return pl.pallas_call(
        functools.partial(_ffn_kernel, cap=cap, nf=nf, bf=bf, n_exp=e),
        grid_spec=grid_spec,
        out_shape=jax.ShapeDtypeStruct((e, cap, d), BF16),
        compiler_params=_params("arbitrary", "arbitrary"),
        name="expert_ffn",
    )(idx.reshape(-1), hn_packed, gate.reshape(e, cap, 1), w1, w3, w2)


def _combine_kernel(lo_ref, h_ref, sr_ref, y_hbm, nw_ref, o_ref,
                    acc_s, pack_s, big_s, sem_pack, sem_big,
                    *, n_exp, n_tiles, cap, ws):
    i = pl.program_id(0)
    tt = h_ref.shape[0]
    n_pack = n_exp * ws

    def bounds(tile, e):
        lo = lo_ref[e * (n_tiles + 1) + tile]
        hi = lo_ref[e * (n_tiles + 1) + tile + 1]
        start = jnp.minimum((lo // BF16_SUBLANES) * BF16_SUBLANES, cap - ws)
        return lo, hi, start

    def pack_copy(tile, e, slot):
        _, _, start = bounds(tile, e)
        start = pl.multiple_of(start, BF16_SUBLANES)
        return pltpu.make_async_copy(y_hbm.at[e, pl.ds(start, ws), :],
                                     pack_s.at[slot, pl.ds(e * ws, ws), :],
                                     sem_pack.at[slot])

    def issue(tile, slot):
        for e in range(n_exp):
            pack_copy(tile, e, slot).start()

    @pl.when(i == 0)
    def _():
        issue(0, 0)

    @pl.when(i + 1 < n_tiles)
    def _():
        issue(i + 1, (i + 1) % 2)

    slot = i % 2
    for e in range(n_exp):
        pack_copy(i, e, slot).wait()

    lane = lax.broadcasted_iota(I32, (1, n_pack), 1)
    target = jnp.full((1, n_pack), -2, I32)
    for e in range(n_exp):
        lo, hi, start = bounds(i, e)
        covered = hi <= start + ws
        first = jnp.where(covered, start, -(cap + n_pack))
        in_e = (lane >= e * ws) & (lane < (e + 1) * ws)
        target = jnp.where(in_e, first + lane - e * ws, target)
    per_tile = LANES // ws
    lane_in = lax.broadcasted_iota(I32, (tt, LANES), 1)
    cols = []
    for p in range(n_pack // LANES):
        e0 = p * per_tile
        blk = jnp.broadcast_to(sr_ref[:, e0:e0 + 1], (tt, LANES))
        for q in range(1, per_tile):
            blk = jnp.where(lane_in >= q * ws, sr_ref[:, e0 + q:e0 + q + 1], blk)
        cols.append(blk)
    sr_wide = jnp.concatenate(cols, axis=1)
    hit = jnp.where(sr_wide == target, 1.0, 0.0).astype(BF16)
    acc_s[...] = h_ref[...] + jnp.dot(hit, pack_s[slot], preferred_element_type=F32)

    wl = big_s.shape[0]
    j_big = lax.broadcasted_iota(I32, (tt, wl), 1)
    for e in range(n_exp):
        lo, hi, start = bounds(i, e)
        start_l = jnp.minimum(start, cap - wl)
        sr = sr_ref[:, e:e + 1]

        def window(begin, first_slot, e=e, sr=sr):
            begin = pl.multiple_of(begin, BF16_SUBLANES)
            cp = pltpu.make_async_copy(y_hbm.at[e, pl.ds(begin, wl), :], big_s, sem_big)
            cp.start()
            cp.wait()
            slot_id = j_big + begin
            hit_l = jnp.where(sr == slot_id, jnp.where(slot_id >= first_slot, 1.0, 0.0), 0.0)
            acc_s[...] += jnp.dot(hit_l.astype(BF16), big_s[...], preferred_element_type=F32)

        @pl.when(hi > start + ws)
        def _():
            window(start_l, lo)

        @pl.when(hi > start_l + wl)
        def _():
            window(jnp.minimum(start_l + wl, cap - wl), start_l + wl)

    x = acc_s[...]
    ms = jnp.mean(x * x, axis=-1, keepdims=True)
    o_ref[...] = x * lax.rsqrt(ms + EPS) * nw_ref[...]


def combine(h, selrank_t, lo_tab, y, norm_w, tt=256, ws=64):
    n, d = h.shape
    e, cap, _ = y.shape
    tt = min(tt, cap, n)
    ws = min(ws, tt)
    assert LANES % ws == 0
    n_tiles = n // tt
    grid_spec = pltpu.PrefetchScalarGridSpec(
        num_scalar_prefetch=1,
        grid=(n_tiles,),
        in_specs=[pl.BlockSpec((tt, d), lambda i, lo: (i, 0)),
                  pl.BlockSpec((tt, e), lambda i, lo: (i, 0)),
                  pl.BlockSpec(memory_space=pl.ANY),
                  pl.BlockSpec((1, d), lambda i, lo: (0, 0))],
        out_specs=pl.BlockSpec((tt, d), lambda i, lo: (i, 0)),
        scratch_shapes=[pltpu.VMEM((tt, d), F32),
                        pltpu.VMEM((2, e * ws, d), BF16),
                        pltpu.VMEM((tt, d), BF16),
                        pltpu.SemaphoreType.DMA((2,)),
                        pltpu.SemaphoreType.DMA(())],
    )
    return pl.pallas_call(
        functools.partial(_combine_kernel, n_exp=e, n_tiles=n_tiles, cap=cap, ws=ws),
        grid_spec=grid_spec,
        out_shape=jax.ShapeDtypeStruct((n, d), F32),
        compiler_params=_params("arbitrary"),
        name="combine",
    )(lo_tab.reshape(-1), h, selrank_t, y, norm_w.reshape(1, d))


def moe_block(h, norm_w, w_router, w1, w3, w2, norm_final_w):
    n, _ = h.shape
    e = w_router.shape[1]
    cap = CAPACITY_FACTOR * n // e
    hn_packed, aff_t = norm_router(h, norm_w, w_router.T)
    selrank, cnt = select_topk(aff_t, cap)
    idx, gate = compact(selrank, cnt, aff_t, cap)
    y = expert_ffn(idx, gate, hn_packed, w1, w3, w2)
    tt = min(256, cap, n)
    lo_tab = jnp.concatenate([cnt[:, ::tt], jnp.full((e, 1), cap, I32)], axis=1)
    return combine(h, selrank.T, lo_tab, y, norm_final_w, tt=tt)


def mixer_block(x, norm_w, w_in, up_f, bias_f, up_b, bias_b, gla_norm_w,
                lb_logits_f, lb_logits_b, hgrn_norm_w, w_out):
    gk = GLA_HEADS * GLA_DK
    gv = GLA_HEADS * GLA_DV
    hd = HGRN_HEADS * HGRN_D
    r = GLA_GATE_RANK
    n_a = 2 * gk + gv
    o_b = n_a + 2 * r
    n_b = w_in.shape[1] - o_b
    wt = jnp.swapaxes(w_in, 0, 1)

    xn = rmsnorm(x, norm_w, BF16)
    proj_a = matmul_nt(xn, wt, 0, n_a, 1024, 512, "inproj_a")
    proj_b = matmul_nt(xn, wt, o_b, n_b, 1024, 512, "inproj_b")
    low = matmul_nt(xn, wt, n_a, LANES, 1024, LANES, "inproj_gate")

    src_gla = {"q": (proj_a, 0), "k": (proj_a, gk), "v": (proj_a, 2 * gk), "og": (proj_b, 0)}
    src_hg = {"q": (proj_b, gv), "v": (proj_b, gv + 3 * hd), "og": (proj_b, gv + 4 * hd)}
    u_f = jnp.zeros((LANES, gk), F32).at[0:r].set(up_f).astype(BF16)
    u_b = jnp.zeros((LANES, gk), F32).at[r:2 * r].set(up_b).astype(BF16)
    gla_f = scan("gla", False, src_gla, {"low": low, "u": u_f, "bias": bias_f.reshape(1, gk)},
                 GLA_HEADS, GLA_DK, GLA_DV)
    gla_o = scan("gla", True, src_gla,
                 {"low": low, "u": u_b, "bias": bias_b.reshape(1, gk), "prev": gla_f,
                  "norm_w": gla_norm_w.reshape(1, GLA_DV)},
                 GLA_HEADS, GLA_DK, GLA_DV)
    hg_f = scan("hgrn", False, dict(src_hg, f=(proj_b, gv + hd)), {"lb_logits": lb_logits_f},
                HGRN_HEADS, HGRN_D, HGRN_D)
    hg_o = scan("hgrn", True, dict(src_hg, f=(proj_b, gv + 2 * hd)),
                {"lb_logits": lb_logits_b, "prev": hg_f,
                 "norm_w": hgrn_norm_w.reshape(1, HGRN_D)},
                HGRN_HEADS, HGRN_D, HGRN_D)
    return outproj(gla_o, hg_o, w_out, x)


def kernel(x, norm_mix_w, w_in, gla_gate_up_f, gla_gate_bias_f, gla_gate_up_b, gla_gate_bias_b, gla_norm_w, hgrn_lb_logits_f, hgrn_lb_logits_b, hgrn_norm_w, w_out, norm_ffn_w, w_router, expert_w1, expert_w3, expert_w2, norm_final_w):
    b, l, d = x.shape
    outs = []
    for bi in range(b):
        xb = x.reshape(l, d) if b == 1 else x[bi]
        h = mixer_block(xb, norm_mix_w[0], w_in[0], gla_gate_up_f[0], gla_gate_bias_f[0],
                        gla_gate_up_b[0], gla_gate_bias_b[0], gla_norm_w[0],
                        hgrn_lb_logits_f, hgrn_lb_logits_b, hgrn_norm_w[0], w_out[0])
        outs.append(moe_block(h, norm_ffn_w[0], w_router[0], expert_w1[0], expert_w3[0],
                              expert_w2[0], norm_final_w))
    return outs[0].reshape(1, l, d) if b == 1 else jnp.stack(outs, axis=0)
```
